```python
import jax, jax.numpy as jnp
from jax import lax
import numpy as np

D_MODEL = 2048
BATCH = 16
SEQ = 2048
DEPTH = 2

CHUNK = 64
LEFT_CHUNKS = 8
BAND_CHUNKS = LEFT_CHUNKS + 1
D_MIX = D_MODEL
D_LRU = D_MIX // 2
LRU_HEADS = 16
LRU_BLOCK = D_LRU // LRU_HEADS
CONV_WIDTH = 4
LRU_C = 8.0
D_ATT = D_MIX - D_LRU
ATT_HEADS = 8
HEAD_DIM = D_ATT // ATT_HEADS
MAX_REL = 128
N_REL = 2 * MAX_REL + 1
D_FF = 5632
D_IN = 2 * D_LRU + 3 * D_ATT
EPS = 1e-6
NEG_INF = -1e30

kernel_name = "macaron_hybrid_rglru_chunkattn"


def rmsnorm(x, g):
    xf = x.astype(jnp.float32)
    y = xf * lax.rsqrt(jnp.mean(xf * xf, axis=-1, keepdims=True) + EPS)
    return (y * g.astype(jnp.float32)).astype(x.dtype)


def swiglu(h, w_gate, w_up, w_down):
    return (jax.nn.silu(h @ w_gate) * (h @ w_up)) @ w_down


def causal_dwconv(x, w, b):
    s = x.shape[1]
    xp = jnp.pad(x, ((0, 0), (CONV_WIDTH - 1, 0), (0, 0)))
    y = b
    for k in range(CONV_WIDTH):
        y = y + xp[:, k:k + s] * w[k]
    return y


def block_diag_linear(x, w, b):
    bsz, s, _ = x.shape
    xh = x.reshape(bsz, s, LRU_HEADS, LRU_BLOCK)
    return jnp.einsum('bshi,hij->bshj', xh, w).reshape(bsz, s, D_LRU) + b


def _lin_rec_combine(e1, e2):
    a1, b1 = e1
    a2, b2 = e2
    return a1 * a2, a2 * b1 + b2


def rg_lru(x, w_a, b_a, w_x, b_x, lam):
    r = jax.nn.sigmoid(block_diag_linear(x, w_a, b_a)).astype(jnp.float32)
    i = jax.nn.sigmoid(block_diag_linear(x, w_x, b_x)).astype(jnp.float32)
    log_a = -LRU_C * r * jax.nn.softplus(-lam.astype(jnp.float32))
    a = jnp.exp(log_a)
    mult = jnp.sqrt(-jnp.expm1(2.0 * log_a))
    u = mult * i * x.astype(jnp.float32)
    _, h = lax.associative_scan(_lin_rec_combine, (a, u), axis=1)
    return h.astype(x.dtype)


def chunk_attention(q, k, v, rel_table):
    bsz, s, _ = q.shape
    nc = s // CHUNK
    q = q.reshape(bsz, nc, CHUNK, ATT_HEADS, HEAD_DIM)
    k = k.reshape(bsz, nc, CHUNK, ATT_HEADS, HEAD_DIM)
    v = v.reshape(bsz, nc, CHUNK, ATT_HEADS, HEAD_DIM)
    pad = ((0, 0), (LEFT_CHUNKS, 0), (0, 0), (0, 0), (0, 0))
    kp = jnp.pad(k, pad)
    vp = jnp.pad(v, pad)
    band_idx = np.arange(nc)[:, None] + np.arange(BAND_CHUNKS)[None, :]
    nk = BAND_CHUNKS * CHUNK
    kb = kp[:, band_idx].reshape(bsz, nc, nk, ATT_HEADS, HEAD_DIM)
    vb = vp[:, band_idx].reshape(bsz, nc, nk, ATT_HEADS, HEAD_DIM)
    scores = jnp.einsum('bcqhd,bckhd->bhcqk', q, kb).astype(jnp.float32) * (HEAD_DIM ** -0.5)
    key_off = (np.arange(BAND_CHUNKS)[:, None] * CHUNK - LEFT_CHUNKS * CHUNK
               + np.arange(CHUNK)[None, :]).reshape(-1)
    rel = key_off[None, :] - np.arange(CHUNK)[:, None]
    rel_idx = np.clip(rel, -MAX_REL, MAX_REL) + MAX_REL
    bias = rel_table[:, rel_idx].astype(jnp.float32)
    valid = (np.arange(nc)[:, None] - LEFT_CHUNKS + np.arange(BAND_CHUNKS)[None, :]) >= 0
    valid = np.repeat(valid, CHUNK, axis=1)
    scores = scores + bias[None, :, None]
    scores = jnp.where(valid[None, None, :, None, :], scores, NEG_INF)
    p = jax.nn.softmax(scores, axis=-1).astype(v.dtype)
    o = jnp.einsum('bhcqk,bckhd->bcqhd', p, vb)
    return o.reshape(bsz, s, D_ATT)


def setup_inputs(seed: int = 0) -> dict:
    key = jax.random.key(seed)
    ks = jax.random.split(key, 24)
    f32 = jnp.float32

    def nrm(k, shape, fan_in):
        return jax.random.normal(k, shape, f32) * (fan_in ** -0.5)

    def gain(k, shape):
        return 1.0 + 0.02 * jax.random.normal(k, shape, f32)

    u = jax.random.uniform(ks[13], (DEPTH, D_LRU), f32, 0.9, 0.999)
    sig = u ** (1.0 / LRU_C)
    lam = jnp.log(sig) - jnp.log1p(-sig)
    return {
        "x": jax.random.normal(ks[0], (BATCH, SEQ, D_MODEL), f32),
        "ffn1_norm": gain(ks[1], (DEPTH, D_MODEL)),
        "ffn1_w_gate": nrm(ks[2], (DEPTH, D_MODEL, D_FF), D_MODEL),
        "ffn1_w_up": nrm(ks[3], (DEPTH, D_MODEL, D_FF), D_MODEL),
        "ffn1_w_down": nrm(ks[4], (DEPTH, D_FF, D_MODEL), D_FF),
        "mix_norm": gain(ks[5], (DEPTH, D_MODEL)),
        "w_in": nrm(ks[6], (DEPTH, D_MODEL, D_IN), D_MODEL),
        "conv_w": nrm(ks[7], (DEPTH, CONV_WIDTH, D_LRU), CONV_WIDTH),
        "conv_b": 0.01 * jax.random.normal(ks[8], (DEPTH, D_LRU), f32),
        "lru_gate_a_w": nrm(ks[9], (DEPTH, LRU_HEADS, LRU_BLOCK, LRU_BLOCK), LRU_BLOCK),
        "lru_gate_a_b": 0.01 * jax.random.normal(ks[10], (DEPTH, D_LRU), f32),
        "lru_gate_x_w": nrm(ks[11], (DEPTH, LRU_HEADS, LRU_BLOCK, LRU_BLOCK), LRU_BLOCK),
        "lru_gate_x_b": 0.01 * jax.random.normal(ks[12], (DEPTH, D_LRU), f32),
        "lru_lambda": lam,
        "rel_bias": 0.1 * jax.random.normal(ks[14], (DEPTH, ATT_HEADS, N_REL), f32),
        "lru_out_norm": gain(ks[15], (DEPTH, D_LRU)),
        "att_out_norm": gain(ks[16], (DEPTH, D_ATT)),
        "w_out": nrm(ks[17], (DEPTH, D_MIX, D_MODEL), D_MIX),
        "ffn2_norm": gain(ks[18], (DEPTH, D_MODEL)),
        "ffn2_w_gate": nrm(ks[19], (DEPTH, D_MODEL, D_FF), D_MODEL),
        "ffn2_w_up": nrm(ks[20], (DEPTH, D_MODEL, D_FF), D_MODEL),
        "ffn2_w_down": nrm(ks[21], (DEPTH, D_FF, D_MODEL), D_FF),
        "final_norm": gain(ks[22], (D_MODEL,)),
    }


def reference(x, ffn1_norm, ffn1_w_gate, ffn1_w_up, ffn1_w_down, mix_norm, w_in,
              conv_w, conv_b, lru_gate_a_w, lru_gate_a_b, lru_gate_x_w, lru_gate_x_b,
              lru_lambda, rel_bias, lru_out_norm, att_out_norm, w_out,
              ffn2_norm, ffn2_w_gate, ffn2_w_up, ffn2_w_down, final_norm):
    splits = [D_LRU, 2 * D_LRU, 2 * D_LRU + D_ATT, 2 * D_LRU + 2 * D_ATT]
    for l in range(DEPTH):
        h = rmsnorm(x, ffn1_norm[l])
        x = x + 0.5 * swiglu(h, ffn1_w_gate[l], ffn1_w_up[l], ffn1_w_down[l])
        h = rmsnorm(x, mix_norm[l])
        z = h @ w_in[l]
        xl, gl, q, k, v = jnp.split(z, splits, axis=-1)
        xl = causal_dwconv(xl, conv_w[l], conv_b[l])
        y_lru = rg_lru(xl, lru_gate_a_w[l], lru_gate_a_b[l], lru_gate_x_w[l],
                       lru_gate_x_b[l], lru_lambda[l]) * jax.nn.gelu(gl)
        y_att = chunk_attention(q, k, v, rel_bias[l])
        y = jnp.concatenate([rmsnorm(y_lru, lru_out_norm[l]),
                             rmsnorm(y_att, att_out_norm[l])], axis=-1)
        x = x + y @ w_out[l]
        h = rmsnorm(x, ffn2_norm[l])
        x = x + 0.5 * swiglu(h, ffn2_w_gate[l], ffn2_w_up[l], ffn2_w_down[l])
    return rmsnorm(x, final_norm)
```

```python
import functools

import jax
import jax.numpy as jnp
import numpy as np
from jax import lax
from jax.experimental import pallas as pl
from jax.experimental.pallas import tpu as pltpu

F32 = jnp.float32
BF16 = jnp.bfloat16

CHUNK = 64
LEFT_CHUNKS = 8
LRU_HEADS = 16
CONV_WIDTH = 4
LRU_C = 8.0
ATT_HEADS = 8
MAX_REL = 128
EPS = 1e-6
NEG_INF = -1e30

LANES = 128
SUBLANES = 8
MXU_DIM = 256
VMEM_LIMIT = 56 * 1024 * 1024

ROW_TILE = 512
FF_TILE = 512
LRU_GROUP = MXU_DIM
LRU_ROWS = 256
ATT_Q = 128


def _params(*sem):
    return pltpu.CompilerParams(dimension_semantics=sem, vmem_limit_bytes=VMEM_LIMIT)


def _rms(x, g):
    ms = jnp.mean(x * x, axis=-1, keepdims=True)
    return x * lax.rsqrt(ms + EPS) * g


def _dot(a, b):
    return jnp.dot(a, b, preferred_element_type=F32)


def _ffn_body(*refs, n_ff, final):
    if final:
        x_ref, g_ref, wg_ref, wu_ref, wd_ref, fg_ref, o_ref, h_ref = refs
    else:
        x_ref, g_ref, wg_ref, wu_ref, wd_ref, o_ref, h_ref = refs
    j = pl.program_id(1)

    @pl.when(j == 0)
    def _():
        h_ref[...] = _rms(x_ref[...], g_ref[...]).astype(BF16)
        o_ref[...] = jnp.zeros_like(o_ref)

    h = h_ref[...]
    g = _dot(h, wg_ref[...])
    u = _dot(h, wu_ref[...])
    a = (g * jax.nn.sigmoid(g) * u).astype(BF16)
    o_ref[...] += _dot(a, wd_ref[...])

    @pl.when(j == n_ff - 1)
    def _():
        y = x_ref[...] + 0.5 * o_ref[...]
        if final:
            y = _rms(y, fg_ref[...])
        o_ref[...] = y


def _ffn(x, gain, wg, wu, wd, final_gain=None):
    t, d = x.shape
    d_ff = wg.shape[1]
    tm = min(ROW_TILE, t)
    tf = min(FF_TILE, d_ff)
    n_ff = d_ff // tf
    final = final_gain is not None
    in_specs = [
        pl.BlockSpec((tm, d), lambda i, j: (i, 0)),
        pl.BlockSpec((1, d), lambda i, j: (0, 0)),
        pl.BlockSpec((d, tf), lambda i, j: (0, j)),
        pl.BlockSpec((d, tf), lambda i, j: (0, j)),
        pl.BlockSpec((tf, d), lambda i, j: (j, 0)),
    ]
    args = [x, gain.reshape(1, d), wg, wu, wd]
    if final:
        in_specs.append(pl.BlockSpec((1, d), lambda i, j: (0, 0)))
        args.append(final_gain.reshape(1, d))
    return pl.pallas_call(
        functools.partial(_ffn_body, n_ff=n_ff, final=final),
        grid=(t // tm, n_ff),
        in_specs=in_specs,
        out_specs=pl.BlockSpec((tm, d), lambda i, j: (i, 0)),
        out_shape=jax.ShapeDtypeStruct((t, d), F32),
        scratch_shapes=[pltpu.VMEM((tm, d), BF16)],
        compiler_params=_params("parallel", "arbitrary"),
        name="ffn_final" if final else "ffn",
    )(*args)


def _inproj_body(x_ref, g_ref, w_ref, xl_ref, gl_ref, q_ref, k_ref, v_ref, h_ref):
    j = pl.program_id(1)

    @pl.when(j == 0)
    def _():
        h_ref[...] = _rms(x_ref[...], g_ref[...]).astype(BF16)

    def proj():
        return _dot(h_ref[...], w_ref[...])

    @pl.when(j == 0)
    def _():
        xl_ref[...] = proj()

    @pl.when(j == 1)
    def _():
        gl_ref[...] = proj()

    for idx, ref in ((2, q_ref), (3, k_ref), (4, v_ref)):
        @pl.when(j == idx)
        def _(ref=ref):
            z = proj().astype(BF16)
            for hd in range(ATT_HEADS):
                ref[0, hd] = z[:, hd * LANES:(hd + 1) * LANES]


def _in_proj(x, gain, w, bsz, seq):
    t, d = x.shape
    width = w.shape[1] // 5
    hd = width // ATT_HEADS
    assert hd == LANES
    tm = min(ROW_TILE, seq)
    nsb = seq // tm
    row = pl.BlockSpec((tm, width), lambda i, j: (i, 0))
    head = pl.BlockSpec((1, ATT_HEADS, tm, hd), lambda i, j: (i // nsb, 0, i % nsb, 0))
    flat = jax.ShapeDtypeStruct((t, width), F32)
    heads = jax.ShapeDtypeStruct((bsz, ATT_HEADS, seq, hd), BF16)
    return pl.pallas_call(
        _inproj_body,
        grid=(t // tm, 5),
        in_specs=[
            pl.BlockSpec((tm, d), lambda i, j: (i, 0)),
            pl.BlockSpec((1, d), lambda i, j: (0, 0)),
            pl.BlockSpec((d, width), lambda i, j: (0, j)),
        ],
        out_specs=[row, row, head, head, head],
        out_shape=[flat, flat, heads, heads, heads],
        scratch_shapes=[pltpu.VMEM((tm, d), BF16)],
        compiler_params=_params("parallel", "arbitrary"),
        name="in_proj",
    )(x, gain.reshape(1, d), w)


def _lru_body(xl_ref, gl_ref, cw_ref, cb_ref, wa_ref, ba_ref, wx_ref, bx_ref, lam_ref,
              y_ref, *, seq, rows):
    c = xl_ref.shape[1]
    cw = cw_ref[...]
    cb = cb_ref[...]
    ba = ba_ref[...]
    bx = bx_ref[...]
    neg_c_sp = -LRU_C * jax.nn.softplus(-lam_ref[...])
    rmod = lax.broadcasted_iota(jnp.int32, (rows, c), 0) % SUBLANES
    row8 = lax.broadcasted_iota(jnp.int32, (SUBLANES, c), 0)

    def chunk(ci, carry):
        x_prev, h_prev = carry
        r0 = pl.multiple_of(ci * rows, rows)
        x = xl_ref[pl.ds(r0, rows), :]

        xc = cb
        for k in range(CONV_WIDTH):
            shift = CONV_WIDTH - 1 - k
            if shift == 0:
                xs = x
            else:
                xs = pltpu.roll(x, shift, 0)
                head = jnp.where(row8 < shift, pltpu.roll(x_prev, shift, 0), xs[0:SUBLANES])
                xs = jnp.concatenate([head, xs[SUBLANES:]], axis=0)
            xc = xc + xs * cw[k:k + 1]

        xb = xc.astype(BF16)
        r = jax.nn.sigmoid(_dot(xb, wa_ref[0]) + ba)
        gi = jax.nn.sigmoid(_dot(xb, wx_ref[0]) + bx)
        log_a = neg_c_sp * r
        a = jnp.exp(log_a)
        mult = jnp.sqrt(-jnp.tanh(log_a) * (1.0 + a * a))
        u = mult * gi * xc

        for k in (1, 2, 4):
            a_s = jnp.where(rmod >= k, pltpu.roll(a, k, 0), 1.0)
            u_s = jnp.where(rmod >= k, pltpu.roll(u, k, 0), 0.0)
            u = a * u_s + u
            a = a * a_s

        hs = []
        h = h_prev
        for t in range(rows // SUBLANES):
            sl = slice(t * SUBLANES, (t + 1) * SUBLANES)
            ht = a[sl] * h + u[sl]
            hs.append(ht)
            h = jnp.broadcast_to(ht[SUBLANES - 1:SUBLANES], (SUBLANES, c))
        hh = jnp.concatenate(hs, axis=0)

        y_ref[pl.ds(r0, rows), :] = hh * jax.nn.gelu(gl_ref[pl.ds(r0, rows), :])
        return x[rows - SUBLANES:rows], h

    zero = jnp.zeros((SUBLANES, c), F32)
    lax.fori_loop(0, seq // rows, chunk, (zero, zero))


def _lru(xl, gl, conv_w, conv_b, wa, ba, wx, bx, lam, bsz, seq):
    t, width = xl.shape
    c = LRU_GROUP
    ng = width // c
    rows = min(LRU_ROWS, seq)
    blk = pl.BlockSpec((seq, c), lambda b, g: (b, g))
    vec = pl.BlockSpec((1, c), lambda b, g: (0, g))
    mat = pl.BlockSpec((1, c, c), lambda b, g: (g, 0, 0))
    return pl.pallas_call(
        functools.partial(_lru_body, seq=seq, rows=rows),
        grid=(bsz, ng),
        in_specs=[blk, blk, pl.BlockSpec((CONV_WIDTH, c), lambda b, g: (0, g)),
                  vec, mat, vec, mat, vec, vec],
        out_specs=blk,
        out_shape=jax.ShapeDtypeStruct((t, width), F32),
        compiler_params=_params("parallel", "parallel"),
        name="lru",
    )(xl, gl, conv_w, conv_b.reshape(1, width), wa, ba.reshape(1, width),
      wx, bx.reshape(1, width), lam.reshape(1, width))


def _block_diag(w):
    heads, n, _ = w.shape
    per = LRU_GROUP // n
    w4 = w.reshape(heads // per, per, n, n)
    eye = jnp.eye(per, dtype=w.dtype)
    bd = jnp.einsum("gaij,ab->gaibj", w4, eye)
    return bd.reshape(heads // per, LRU_GROUP, LRU_GROUP).astype(BF16)


def _attn_body(q_ref, k_ref, v_ref, bias_ref, o_ref, kp_ref, vp_ref, *, seq, mq):
    pad = LEFT_CHUNKS * CHUNK
    win = pad + mq
    hd = q_ref.shape[-1]
    scale = hd ** -0.5
    kp_ref[0:pad, :] = jnp.zeros((pad, hd), BF16)
    vp_ref[0:pad, :] = jnp.zeros((pad, hd), BF16)
    kp_ref[pad:pad + seq, :] = k_ref[0, 0]
    vp_ref[pad:pad + seq, :] = v_ref[0, 0]
    col = lax.broadcasted_iota(jnp.int32, (mq, win), 1)

    def block(qi, _):
        q0 = pl.multiple_of(qi * mq, mq)
        q = q_ref[0, 0, pl.ds(q0, mq), :]
        kw = kp_ref[pl.ds(q0, win), :]
        vw = vp_ref[pl.ds(q0, win), :]
        s = lax.dot_general(q, kw, (((1,), (1,)), ((), ())), preferred_element_type=F32)
        s = s * scale + bias_ref[0]
        s = jnp.where(col >= pad - q0, s, NEG_INF)
        m = jnp.max(s, axis=-1, keepdims=True)
        p = jnp.exp(s - m)
        l = jnp.sum(p, axis=-1, keepdims=True)
        o = _dot(p.astype(BF16), vw) / l
        o_ref[0, 0, pl.ds(q0, mq), :] = o
        return 0

    lax.fori_loop(0, seq // mq, block, 0)


def _attn_bias(rel_table, mq):
    pad = LEFT_CHUNKS * CHUNK
    qpos = np.arange(mq)[:, None]
    kpos = np.arange(pad + mq)[None, :] - pad
    rel_idx = np.clip(kpos - qpos, -MAX_REL, MAX_REL) + MAX_REL
    qc = qpos // CHUNK
    kc = np.floor_divide(kpos, CHUNK)
    band = (kc <= qc) & (kc >= qc - LEFT_CHUNKS)
    bias = rel_table[:, rel_idx].astype(F32)
    return jnp.where(band[None], bias, NEG_INF)


def _attn(q, k, v, rel_table):
    bsz, heads, seq, hd = q.shape
    mq = min(ATT_Q, seq)
    pad = LEFT_CHUNKS * CHUNK
    bias = _attn_bias(rel_table, mq)
    blk = pl.BlockSpec((1, 1, seq, hd), lambda b, h: (b, h, 0, 0))
    return pl.pallas_call(
        functools.partial(_attn_body, seq=seq, mq=mq),
        grid=(bsz, heads),
        in_specs=[blk, blk, blk, pl.BlockSpec((1, mq, pad + mq), lambda b, h: (h, 0, 0))],
        out_specs=blk,
        out_shape=jax.ShapeDtypeStruct((bsz, heads, seq, hd), F32),
        scratch_shapes=[pltpu.VMEM((pad + seq, hd), BF16), pltpu.VMEM((pad + seq, hd), BF16)],
        compiler_params=_params("parallel", "parallel"),
        name="attn",
    )(q, k, v, bias)


def _outproj_body(yl_ref, ya_ref, gl_ref, ga_ref, w_ref, x_ref, o_ref):
    half = yl_ref.shape[1]
    yl = _rms(yl_ref[...], gl_ref[...]).astype(BF16)
    att = jnp.concatenate([ya_ref[0, hd] for hd in range(ATT_HEADS)], axis=-1)
    ya = _rms(att, ga_ref[...]).astype(BF16)
    y = _dot(yl, w_ref[0:half, :]) + _dot(ya, w_ref[half:2 * half, :])
    o_ref[...] = x_ref[...] + y


def _out_proj(y_lru, y_att, g_lru, g_att, w, x):
    t, d = x.shape
    half = y_lru.shape[1]
    bsz, heads, seq, hd = y_att.shape
    tm = min(ROW_TILE, seq)
    nsb = seq // tm
    return pl.pallas_call(
        _outproj_body,
        grid=(t // tm,),
        in_specs=[
            pl.BlockSpec((tm, half), lambda i: (i, 0)),
            pl.BlockSpec((1, heads, tm, hd), lambda i: (i // nsb, 0, i % nsb, 0)),
            pl.BlockSpec((1, half), lambda i: (0, 0)),
            pl.BlockSpec((1, half), lambda i: (0, 0)),
            pl.BlockSpec((2 * half, d), lambda i: (0, 0)),
            pl.BlockSpec((tm, d), lambda i: (i, 0)),
        ],
        out_specs=pl.BlockSpec((tm, d), lambda i: (i, 0)),
        out_shape=jax.ShapeDtypeStruct((t, d), F32),
        compiler_params=_params("parallel"),
        name="out_proj",
    )(y_lru, y_att, g_lru.reshape(1, half), g_att.reshape(1, half), w, x)


def kernel(x, ffn1_norm, ffn1_w_gate, ffn1_w_up, ffn1_w_down, mix_norm, w_in, conv_w, conv_b, lru_gate_a_w, lru_gate_a_b, lru_gate_x_w, lru_gate_x_b, lru_lambda, rel_bias, lru_out_norm, att_out_norm, w_out, ffn2_norm, ffn2_w_gate, ffn2_w_up, ffn2_w_down, final_norm):
    bsz, seq, d = x.shape
    depth = w_in.shape[0]
    xf = x.reshape(bsz * seq, d)
    bf = lambda w: w.astype(BF16)
    wg1, wu1, wd1 = bf(ffn1_w_gate), bf(ffn1_w_up), bf(ffn1_w_down)
    wg2, wu2, wd2 = bf(ffn2_w_gate), bf(ffn2_w_up), bf(ffn2_w_down)
    w_in_b, w_out_b = bf(w_in), bf(w_out)
    for l in range(depth):
        xf = _ffn(xf, ffn1_norm[l], wg1[l], wu1[l], wd1[l])
        xl, gl, q, k, v = _in_proj(xf, mix_norm[l], w_in_b[l], bsz, seq)
        y_lru = _lru(xl, gl, conv_w[l], conv_b[l],
                     _block_diag(lru_gate_a_w[l]), lru_gate_a_b[l],
                     _block_diag(lru_gate_x_w[l]), lru_gate_x_b[l],
                     lru_lambda[l], bsz, seq)
        y_att = _attn(q, k, v, rel_bias[l])
        xf = _out_proj(y_lru, y_att, lru_out_norm[l], att_out_norm[l], w_out_b[l], xf)
        last = l == depth - 1
        xf = _ffn(xf, ffn2_norm[l], wg2[l], wu2[l], wd2[l],
                  final_gain=final_norm if last else None)
    return xf.reshape(bsz, seq, d)
```

```python
import functools

import jax
import jax.numpy as jnp
from jax import lax
from jax.experimental import pallas as pl
from jax.experimental.pallas import tpu as pltpu

F32 = jnp.float32
BF16 = jnp.bfloat16

CHUNK = 64
LEFT_CHUNKS = 8
LRU_HEADS = 16
CONV_WIDTH = 4
LRU_C = 8.0
ATT_HEADS = 8
MAX_REL = 128
EPS = 1e-6
NEG_INF = -1e30

LANES = 128
SUBLANES = 8
MXU_DIM = 256
VMEM_LIMIT = 56 * 1024 * 1024

ROW_TILE = 512
FF_TILE = 512
FF_SPLIT = 2
LRU_GROUP = MXU_DIM
LRU_ROWS = 256
ATT_Q = 256
ATT_UNROLL = 4


def _params(*sem, flags=None):
    return pltpu.CompilerParams(dimension_semantics=sem, vmem_limit_bytes=VMEM_LIMIT, flags=flags)


def _rms(x, g):
    ms = jnp.mean(x * x, axis=-1, keepdims=True)
    return x * lax.rsqrt(ms + EPS) * g


def _dot(a, b):
    return jnp.dot(a, b, preferred_element_type=F32)


def _layer_vec(width, layer):
    return pl.BlockSpec((None, 1, width), lambda *_: (layer, 0, 0))


def _ffn_body(*refs, n_ff, final):
    if final:
        x_ref, g_ref, wg_ref, wu_ref, wd_ref, fg_ref, o_ref, h_ref = refs
    else:
        x_ref, g_ref, wg_ref, wu_ref, wd_ref, o_ref, h_ref = refs
    j = pl.program_id(1)

    @pl.when(j == 0)
    def _():
        x = x_ref[...]
        h_ref[...] = _rms(x, g_ref[...]).astype(BF16)
        o_ref[...] = x

    h = h_ref[...]
    ts = wg_ref.shape[1] // FF_SPLIT
    acc = None
    for s in range(FF_SPLIT):
        g = _dot(h, wg_ref[:, s * ts:(s + 1) * ts])
        u = _dot(h, wu_ref[:, s * ts:(s + 1) * ts])
        a = (g * jax.nn.sigmoid(g) * (0.5 * u)).astype(BF16)
        d = _dot(a, wd_ref[s * ts:(s + 1) * ts, :])
        acc = d if acc is None else acc + d
    o_ref[...] += acc

    if final:
        @pl.when(j == n_ff - 1)
        def _():
            o_ref[...] = _rms(o_ref[...], fg_ref[...])


def _ffn(x, gain, wg, wu, wd, layer, final_gain=None):
    t, d = x.shape
    d_ff = wg.shape[2]
    tm = min(ROW_TILE, t)
    tf = min(FF_TILE, d_ff)
    n_ff = d_ff // tf
    final = final_gain is not None
    in_specs = [
        pl.BlockSpec((tm, d), lambda i, j: (i, 0)),
        _layer_vec(d, layer),
        pl.BlockSpec((None, d, tf), lambda i, j: (layer, 0, j)),
        pl.BlockSpec((None, d, tf), lambda i, j: (layer, 0, j)),
        pl.BlockSpec((None, tf, d), lambda i, j: (layer, j, 0)),
    ]
    args = [x, gain, wg, wu, wd]
    if final:
        in_specs.append(pl.BlockSpec((1, d), lambda i, j: (0, 0)))
        args.append(final_gain.reshape(1, d))
    return pl.pallas_call(
        functools.partial(_ffn_body, n_ff=n_ff, final=final),
        grid=(t // tm, n_ff),
        in_specs=in_specs,
        out_specs=pl.BlockSpec((tm, d), lambda i, j: (i, 0)),
        out_shape=jax.ShapeDtypeStruct((t, d), F32),
        scratch_shapes=[pltpu.VMEM((tm, d), BF16)],
        compiler_params=_params("parallel", "arbitrary"),
        name="ffn_final" if final else "ffn",
    )(*args)


def _inproj_body(x_ref, g_ref, w_ref, xl_ref, gl_ref, q_ref, k_ref, v_ref, h_ref):
    width = xl_ref.shape[1]
    h_ref[...] = _rms(x_ref[...], g_ref[...]).astype(BF16)

    def proj(idx):
        return _dot(h_ref[...], w_ref[:, idx * width:(idx + 1) * width])

    xl_ref[...] = proj(0)
    gl_ref[...] = proj(1)
    for idx, ref in ((2, q_ref), (3, k_ref), (4, v_ref)):
        z = proj(idx).astype(BF16)
        for hd in range(ATT_HEADS):
            ref[0, hd] = z[:, hd * LANES:(hd + 1) * LANES]


def _in_proj(x, gain, w, layer, bsz, seq):
    t, d = x.shape
    width = w.shape[2] // 5
    hd = width // ATT_HEADS
    assert hd == LANES
    tm = min(ROW_TILE, seq)
    nsb = seq // tm
    row = pl.BlockSpec((tm, width), lambda i: (i, 0))
    head = pl.BlockSpec((1, ATT_HEADS, tm, hd), lambda i: (i // nsb, 0, i % nsb, 0))
    flat = jax.ShapeDtypeStruct((t, width), F32)
    heads = jax.ShapeDtypeStruct((bsz, ATT_HEADS, seq, hd), BF16)
    return pl.pallas_call(
        _inproj_body,
        grid=(t // tm,),
        in_specs=[
            pl.BlockSpec((tm, d), lambda i: (i, 0)),
            _layer_vec(d, layer),
            pl.BlockSpec((None, d, 5 * width), lambda i: (layer, 0, 0),
                         pipeline_mode=pl.Buffered(1)),
        ],
        out_specs=[row, row, head, head, head],
        out_shape=[flat, flat, heads, heads, heads],
        scratch_shapes=[pltpu.VMEM((tm, d), BF16)],
        compiler_params=_params("parallel"),
        name="in_proj",
    )(x, gain, w)


def _lru_body(xl_ref, gl_ref, cw_ref, cb_ref, wa_ref, ba_ref, wx_ref, bx_ref, lam_ref,
              y_ref, *, seq, rows):
    c = xl_ref.shape[1]
    cw = cw_ref[...]
    cb = cb_ref[...]
    ba = ba_ref[...]
    bx = bx_ref[...]
    neg_c_sp = -LRU_C * jax.nn.softplus(-lam_ref[...])
    rmod = lax.broadcasted_iota(jnp.int32, (rows, c), 0) % SUBLANES
    row8 = lax.broadcasted_iota(jnp.int32, (SUBLANES, c), 0)

    def chunk(ci, carry):
        x_prev, h_prev = carry
        r0 = pl.multiple_of(ci * rows, rows)
        x = xl_ref[pl.ds(r0, rows), :]

        xc = cb
        for k in range(CONV_WIDTH):
            shift = CONV_WIDTH - 1 - k
            if shift == 0:
                xs = x
            else:
                xs = pltpu.roll(x, shift, 0)
                head = jnp.where(row8 < shift, pltpu.roll(x_prev, shift, 0), xs[0:SUBLANES])
                xs = jnp.concatenate([head, xs[SUBLANES:]], axis=0)
            xc = xc + xs * cw[k:k + 1]

        xb = xc.astype(BF16)
        r = jax.nn.sigmoid(_dot(xb, wa_ref[...]) + ba)
        gi = jax.nn.sigmoid(_dot(xb, wx_ref[...]) + bx)
        log_a = neg_c_sp * r
        a = jnp.exp(log_a)
        mult = jnp.sqrt(-jnp.tanh(log_a) * (1.0 + a * a))
        u = mult * gi * xc

        for k in (1, 2, 4):
            a_s = jnp.where(rmod >= k, pltpu.roll(a, k, 0), 1.0)
            u_s = jnp.where(rmod >= k, pltpu.roll(u, k, 0), 0.0)
            u = a * u_s + u
            a = a * a_s

        hs = []
        h = h_prev
        for t in range(rows // SUBLANES):
            sl = slice(t * SUBLANES, (t + 1) * SUBLANES)
            ht = a[sl] * h + u[sl]
            hs.append(ht)
            h = jnp.broadcast_to(ht[SUBLANES - 1:SUBLANES], (SUBLANES, c))
        hh = jnp.concatenate(hs, axis=0)

        y_ref[pl.ds(r0, rows), :] = hh * jax.nn.gelu(gl_ref[pl.ds(r0, rows), :])
        return x[rows - SUBLANES:rows], h

    zero = jnp.zeros((SUBLANES, c), F32)
    lax.fori_loop(0, seq // rows, chunk, (zero, zero))


def _lru(xl, gl, conv_w, conv_b, wa, ba, wx, bx, lam, layer, bsz, seq):
    t, width = xl.shape
    c = LRU_GROUP
    ng = width // c
    rows = min(LRU_ROWS, seq)
    blk = pl.BlockSpec((seq, c), lambda b, g: (b, g))
    vec = pl.BlockSpec((None, 1, c), lambda b, g: (layer, 0, g))
    mat = pl.BlockSpec((None, None, c, c), lambda b, g: (layer, g, 0, 0))
    return pl.pallas_call(
        functools.partial(_lru_body, seq=seq, rows=rows),
        grid=(bsz, ng),
        in_specs=[blk, blk, pl.BlockSpec((None, CONV_WIDTH, c), lambda b, g: (layer, 0, g)),
                  vec, mat, vec, mat, vec, vec],
        out_specs=blk,
        out_shape=jax.ShapeDtypeStruct((t, width), F32),
        compiler_params=_params("parallel", "parallel"),
        name="lru",
    )(xl, gl, conv_w, conv_b, wa, ba, wx, bx, lam)


def _block_diag(w):
    depth, heads, n, _ = w.shape
    per = LRU_GROUP // n
    w5 = w.reshape(depth, heads // per, per, n, n)
    eye = jnp.eye(per, dtype=w.dtype)
    bd = w5[:, :, :, :, None, :] * eye[None, None, :, None, :, None]
    return bd.reshape(depth, heads // per, LRU_GROUP, LRU_GROUP).astype(BF16)


def _attn_body(q_ref, k_ref, v_ref, e_ref, o_ref, kp_ref, vp_ref, bias_ref, *, seq, mq, unroll):
    pad = LEFT_CHUNKS * CHUNK
    win = pad + mq
    near = MAX_REL + mq
    hd = q_ref.shape[-1]
    scale = hd ** -0.5

    @pl.when(pl.program_id(1) == 0)
    def _():
        e = e_ref[...]
        shifted = pltpu.roll(jnp.broadcast_to(e, (mq, 2 * near)), 0, 1, stride=1, stride_axis=0)
        far = jnp.broadcast_to(e[:, 0:1], (mq, win - near))
        bias = jnp.concatenate([far, shifted[:, near:]], axis=1)
        qc = lax.broadcasted_iota(jnp.int32, (mq, win), 0) // CHUNK
        kc = lax.broadcasted_iota(jnp.int32, (mq, win), 1) // CHUNK - LEFT_CHUNKS
        band = (kc <= qc) & (kc >= qc - LEFT_CHUNKS)
        bias_ref[...] = jnp.where(band, bias, NEG_INF)

    kp_ref[0:pad, :] = jnp.zeros((pad, hd), BF16)
    vp_ref[0:pad, :] = jnp.zeros((pad, hd), BF16)
    kp_ref[pad:pad + seq, :] = k_ref[0, 0]
    vp_ref[pad:pad + seq, :] = v_ref[0, 0]

    def block(q0, masked):
        q = q_ref[0, 0, pl.ds(q0, mq), :]
        kw = kp_ref[pl.ds(q0, win), :]
        vw = vp_ref[pl.ds(q0, win), :]
        s = lax.dot_general(q, kw, (((1,), (1,)), ((), ())), preferred_element_type=F32)
        s = s * scale + bias_ref[...]
        if masked:
            col = lax.broadcasted_iota(jnp.int32, (mq, win), 1)
            s = jnp.where(col >= pad - q0, s, NEG_INF)
        m = jnp.max(s, axis=-1, keepdims=True)
        p = jnp.exp(s - m)
        l = jnp.sum(p, axis=-1, keepdims=True)
        o = _dot(p.astype(BF16), vw) / l
        o_ref[0, 0, pl.ds(q0, mq), :] = o

    nblk = seq // mq
    n_static = min(-(-(pad // mq) // unroll) * unroll, nblk)
    for u in range(n_static):
        block(u * mq, masked=u * mq < pad)

    def group(gi, _):
        for u in range(unroll):
            block(pl.multiple_of((gi * unroll + u) * mq, mq), masked=False)
        return 0

    lax.fori_loop(n_static // unroll, nblk // unroll, group, 0)


def _attn_table(rel_table, mq):
    near = MAX_REL + mq
    n_rel = rel_table.shape[-1]
    lead = jnp.broadcast_to(rel_table[..., :1], rel_table.shape[:-1] + (near,))
    body = rel_table[..., :min(near, n_rel)]
    parts = [lead, body]
    if near > n_rel:
        parts.append(jnp.broadcast_to(rel_table[..., -1:], rel_table.shape[:-1] + (near - n_rel,)))
    return jnp.concatenate(parts, axis=-1)[:, :, None, :]


def _attn(q, k, v, table, layer):
    bsz, heads, seq, hd = q.shape
    mq = min(ATT_Q, seq)
    unroll = min(ATT_UNROLL, seq // mq)
    pad = LEFT_CHUNKS * CHUNK
    blk = pl.BlockSpec((1, 1, seq, hd), lambda h, b: (b, h, 0, 0))
    return pl.pallas_call(
        functools.partial(_attn_body, seq=seq, mq=mq, unroll=unroll),
        grid=(heads, bsz),
        in_specs=[blk, blk, blk,
                  pl.BlockSpec((None, None, 1, table.shape[-1]), lambda h, b: (layer, h, 0, 0))],
        out_specs=blk,
        out_shape=jax.ShapeDtypeStruct((bsz, heads, seq, hd), F32),
        scratch_shapes=[pltpu.VMEM((pad + seq, hd), BF16), pltpu.VMEM((pad + seq, hd), BF16),
                        pltpu.VMEM((mq, pad + mq), F32)],
        compiler_params=_params("parallel", "arbitrary"),
        name="attn",
    )(q, k, v, table)


def _outproj_body(yl_ref, ya_ref, gl_ref, ga_ref, w_ref, x_ref, o_ref):
    half = yl_ref.shape[1]
    yl = _rms(yl_ref[...], gl_ref[...]).astype(BF16)
    att = jnp.concatenate([ya_ref[0, hd] for hd in range(ATT_HEADS)], axis=-1)
    ya = _rms(att, ga_ref[...]).astype(BF16)
    y = _dot(yl, w_ref[0:half, :]) + _dot(ya, w_ref[half:2 * half, :])
    o_ref[...] = x_ref[...] + y


def _out_proj(y_lru, y_att, g_lru, g_att, w, x, layer):
    t, d = x.shape
    half = y_lru.shape[1]
    bsz, heads, seq, hd = y_att.shape
    tm = min(ROW_TILE, seq)
    nsb = seq // tm
    return pl.pallas_call(
        _outproj_body,
        grid=(t // tm,),
        in_specs=[
            pl.BlockSpec((tm, half), lambda i: (i, 0)),
            pl.BlockSpec((1, heads, tm, hd), lambda i: (i // nsb, 0, i % nsb, 0)),
            _layer_vec(half, layer),
            _layer_vec(half, layer),
            pl.BlockSpec((None, 2 * half, d), lambda i: (layer, 0, 0),
                         pipeline_mode=pl.Buffered(1)),
            pl.BlockSpec((tm, d), lambda i: (i, 0)),
        ],
        out_specs=pl.BlockSpec((tm, d), lambda i: (i, 0)),
        out_shape=jax.ShapeDtypeStruct((t, d), F32),
        compiler_params=_params("parallel"),
        name="out_proj",
    )(y_lru, y_att, g_lru, g_att, w, x)


def kernel(x, ffn1_norm, ffn1_w_gate, ffn1_w_up, ffn1_w_down, mix_norm, w_in, conv_w, conv_b, lru_gate_a_w, lru_gate_a_b, lru_gate_x_w, lru_gate_x_b, lru_lambda, rel_bias, lru_out_norm, att_out_norm, w_out, ffn2_norm, ffn2_w_gate, ffn2_w_up, ffn2_w_down, final_norm):
    bsz, seq, d = x.shape
    depth = w_in.shape[0]
    xf = x.reshape(bsz * seq, d)
    bf = lambda w: w.astype(BF16)
    vec = lambda p: p[:, None, :]
    wg1, wu1, wd1 = bf(ffn1_w_gate), bf(ffn1_w_up), bf(ffn1_w_down)
    wg2, wu2, wd2 = bf(ffn2_w_gate), bf(ffn2_w_up), bf(ffn2_w_down)
    w_in_b, w_out_b = bf(w_in), bf(w_out)
    wa, wx = _block_diag(lru_gate_a_w), _block_diag(lru_gate_x_w)
    table = _attn_table(rel_bias, min(ATT_Q, seq))
    n1, nm, n2 = vec(ffn1_norm), vec(mix_norm), vec(ffn2_norm)
    cb, ba, bx, lam = vec(conv_b), vec(lru_gate_a_b), vec(lru_gate_x_b), vec(lru_lambda)
    g_lru, g_att = vec(lru_out_norm), vec(att_out_norm)
    for l in range(depth):
        xf = _ffn(xf, n1, wg1, wu1, wd1, l)
        xl, gl, q, k, v = _in_proj(xf, nm, w_in_b, l, bsz, seq)
        y_lru = _lru(xl, gl, conv_w, cb, wa, ba, wx, bx, lam, l, bsz, seq)
        y_att = _attn(q, k, v, table, l)
        xf = _out_proj(y_lru, y_att, g_lru, g_att, w_out_b, xf, l)
        last = l == depth - 1
        xf = _ffn(xf, n2, wg2, wu2, wd2, l, final_gain=final_norm if last else None)
    return xf.reshape(bsz, seq, d)
```

```python
import functools

import jax
import jax.numpy as jnp
from jax import lax
from jax.experimental import pallas as pl
from jax.experimental.pallas import tpu as pltpu

F32 = jnp.float32
BF16 = jnp.bfloat16

CHUNK = 64
LEFT_CHUNKS = 8
LRU_HEADS = 16
CONV_WIDTH = 4
LRU_C = 8.0
ATT_HEADS = 8
MAX_REL = 128
EPS = 1e-6
NEG_INF = -1e30

LANES = 128
SUBLANES = 8
MXU_DIM = 256
VMEM_LIMIT = 56 * 1024 * 1024

ROW_TILE = 512
FF_ROW_TILE = 1024
FF_TILE = 512
FF_SPLIT = 2
LRU_GROUP = MXU_DIM
LRU_ROWS = 256
ATT_Q = 256
ATT_UNROLL = 4


def _params(*sem, flags=None):
    return pltpu.CompilerParams(dimension_semantics=sem, vmem_limit_bytes=VMEM_LIMIT, flags=flags)


def _rms(x, g):
    ms = jnp.mean(x * x, axis=-1, keepdims=True)
    return x * lax.rsqrt(ms + EPS) * g


def _dot(a, b):
    return jnp.dot(a, b, preferred_element_type=F32)


def _layer_vec(width, layer):
    return pl.BlockSpec((None, 1, width), lambda *_: (layer, 0, 0))


def _ffn_body(*refs, n_ff, final):
    if final:
        x_ref, g_ref, wg_ref, wu_ref, wd_ref, fg_ref, o_ref, h_ref = refs
    else:
        x_ref, g_ref, wg_ref, wu_ref, wd_ref, o_ref, h_ref = refs
    j = pl.program_id(1)

    @pl.when(j == 0)
    def _():
        x = x_ref[...]
        h_ref[...] = _rms(x, g_ref[...]).astype(BF16)
        o_ref[...] = x

    h = h_ref[...]
    ts = wg_ref.shape[1] // FF_SPLIT
    for s in range(FF_SPLIT):
        g = _dot(h, wg_ref[:, s * ts:(s + 1) * ts])
        u = _dot(h, wu_ref[:, s * ts:(s + 1) * ts])
        a = (g * jax.nn.sigmoid(g) * (0.5 * u)).astype(BF16)
        o_ref[...] += _dot(a, wd_ref[s * ts:(s + 1) * ts, :])

    if final:
        @pl.when(j == n_ff - 1)
        def _():
            o_ref[...] = _rms(o_ref[...], fg_ref[...])


def _ffn(x, gain, wg, wu, wd, layer, final_gain=None):
    t, d = x.shape
    d_ff = wg.shape[2]
    tm = min(FF_ROW_TILE, t)
    tf = min(FF_TILE, d_ff)
    n_ff = d_ff // tf
    final = final_gain is not None
    in_specs = [
        pl.BlockSpec((tm, d), lambda i, j: (i, 0)),
        _layer_vec(d, layer),
        pl.BlockSpec((None, d, tf), lambda i, j: (layer, 0, j)),
        pl.BlockSpec((None, d, tf), lambda i, j: (layer, 0, j)),
        pl.BlockSpec((None, tf, d), lambda i, j: (layer, j, 0)),
    ]
    args = [x, gain, wg, wu, wd]
    if final:
        in_specs.append(pl.BlockSpec((1, d), lambda i, j: (0, 0)))
        args.append(final_gain.reshape(1, d))
    return pl.pallas_call(
        functools.partial(_ffn_body, n_ff=n_ff, final=final),
        grid=(t // tm, n_ff),
        in_specs=in_specs,
        out_specs=pl.BlockSpec((tm, d), lambda i, j: (i, 0)),
        out_shape=jax.ShapeDtypeStruct((t, d), F32),
        scratch_shapes=[pltpu.VMEM((tm, d), BF16)],
        compiler_params=_params("parallel", "arbitrary"),
        name="ffn_final" if final else "ffn",
    )(*args)


def _inproj_body(x_ref, g_ref, w_ref, xl_ref, gl_ref, q_ref, k_ref, v_ref, h_ref):
    width = xl_ref.shape[1]
    h_ref[...] = _rms(x_ref[...], g_ref[...]).astype(BF16)

    def proj(idx):
        return _dot(h_ref[...], w_ref[:, idx * width:(idx + 1) * width])

    xl_ref[...] = proj(0)
    gl_ref[...] = proj(1)
    for idx, ref in ((2, q_ref), (3, k_ref), (4, v_ref)):
        z = proj(idx).astype(BF16)
        for hd in range(ATT_HEADS):
            ref[0, hd] = z[:, hd * LANES:(hd + 1) * LANES]


def _in_proj(x, gain, w, layer, bsz, seq):
    t, d = x.shape
    width = w.shape[2] // 5
    hd = width // ATT_HEADS
    assert hd == LANES
    tm = min(ROW_TILE, seq)
    nsb = seq // tm
    row = pl.BlockSpec((tm, width), lambda i: (i, 0))
    head = pl.BlockSpec((1, ATT_HEADS, tm, hd), lambda i: (i // nsb, 0, i % nsb, 0))
    flat = jax.ShapeDtypeStruct((t, width), F32)
    heads = jax.ShapeDtypeStruct((bsz, ATT_HEADS, seq, hd), BF16)
    return pl.pallas_call(
        _inproj_body,
        grid=(t // tm,),
        in_specs=[
            pl.BlockSpec((tm, d), lambda i: (i, 0)),
            _layer_vec(d, layer),
            pl.BlockSpec((None, d, 5 * width), lambda i: (layer, 0, 0),
                         pipeline_mode=pl.Buffered(1)),
        ],
        out_specs=[row, row, head, head, head],
        out_shape=[flat, flat, heads, heads, heads],
        scratch_shapes=[pltpu.VMEM((tm, d), BF16)],
        compiler_params=_params("parallel"),
        name="in_proj",
    )(x, gain, w)


def _lru_body(xl_ref, gl_ref, cw_ref, cb_ref, wa_ref, ba_ref, wx_ref, bx_ref, lam_ref,
              y_ref, *, seq, rows):
    c = xl_ref.shape[1]
    cw = cw_ref[...]
    cb = cb_ref[...]
    ba = ba_ref[...]
    bx = bx_ref[...]
    neg_c_sp = -LRU_C * jax.nn.softplus(-lam_ref[...])
    rmod = lax.broadcasted_iota(jnp.int32, (rows, c), 0) % SUBLANES
    row8 = lax.broadcasted_iota(jnp.int32, (SUBLANES, c), 0)

    def chunk(ci, carry):
        x_prev, h_prev = carry
        r0 = pl.multiple_of(ci * rows, rows)
        x = xl_ref[pl.ds(r0, rows), :]

        xc = cb
        for k in range(CONV_WIDTH):
            shift = CONV_WIDTH - 1 - k
            if shift == 0:
                xs = x
            else:
                xs = pltpu.roll(x, shift, 0)
                head = jnp.where(row8 < shift, pltpu.roll(x_prev, shift, 0), xs[0:SUBLANES])
                xs = jnp.concatenate([head, xs[SUBLANES:]], axis=0)
            xc = xc + xs * cw[k:k + 1]

        xb = xc.astype(BF16)
        r = jax.nn.sigmoid(_dot(xb, wa_ref[...]) + ba)
        gi = jax.nn.sigmoid(_dot(xb, wx_ref[...]) + bx)
        log_a = neg_c_sp * r
        a = jnp.exp(log_a)
        mult = jnp.sqrt(-jnp.tanh(log_a) * (1.0 + a * a))
        u = mult * gi * xc

        for k in (1, 2, 4):
            a_s = jnp.where(rmod >= k, pltpu.roll(a, k, 0), 1.0)
            u_s = jnp.where(rmod >= k, pltpu.roll(u, k, 0), 0.0)
            u = a * u_s + u
            a = a * a_s

        hs = []
        h = h_prev
        for t in range(rows // SUBLANES):
            sl = slice(t * SUBLANES, (t + 1) * SUBLANES)
            ht = a[sl] * h + u[sl]
            hs.append(ht)
            h = jnp.broadcast_to(ht[SUBLANES - 1:SUBLANES], (SUBLANES, c))
        hh = jnp.concatenate(hs, axis=0)

        y_ref[pl.ds(r0, rows), :] = hh * jax.nn.gelu(gl_ref[pl.ds(r0, rows), :])
        return x[rows - SUBLANES:rows], h

    zero = jnp.zeros((SUBLANES, c), F32)
    lax.fori_loop(0, seq // rows, chunk, (zero, zero))


def _lru(xl, gl, conv_w, conv_b, wa, ba, wx, bx, lam, layer, bsz, seq):
    t, width = xl.shape
    c = LRU_GROUP
    ng = width // c
    rows = min(LRU_ROWS, seq)
    blk = pl.BlockSpec((seq, c), lambda b, g: (b, g))
    vec = pl.BlockSpec((None, 1, c), lambda b, g: (layer, 0, g))
    mat = pl.BlockSpec((None, None, c, c), lambda b, g: (layer, g, 0, 0))
    return pl.pallas_call(
        functools.partial(_lru_body, seq=seq, rows=rows),
        grid=(bsz, ng),
        in_specs=[blk, blk, pl.BlockSpec((None, CONV_WIDTH, c), lambda b, g: (layer, 0, g)),
                  vec, mat, vec, mat, vec, vec],
        out_specs=blk,
        out_shape=jax.ShapeDtypeStruct((t, width), F32),
        compiler_params=_params("parallel", "parallel"),
        name="lru",
    )(xl, gl, conv_w, conv_b, wa, ba, wx, bx, lam)


def _block_diag(w):
    depth, heads, n, _ = w.shape
    per = LRU_GROUP // n
    w5 = w.reshape(depth, heads // per, per, n, n)
    eye = jnp.eye(per, dtype=w.dtype)
    bd = w5[:, :, :, :, None, :] * eye[None, None, :, None, :, None]
    return bd.reshape(depth, heads // per, LRU_GROUP, LRU_GROUP).astype(BF16)


def _attn_body(q_ref, k_ref, v_ref, e_ref, o_ref, kp_ref, vp_ref, bias_ref, *, seq, mq, unroll):
    pad = LEFT_CHUNKS * CHUNK
    win = pad + mq
    near = MAX_REL + mq
    hd = q_ref.shape[-1]
    scale = hd ** -0.5

    @pl.when(pl.program_id(1) == 0)
    def _():
        e = e_ref[...]
        shifted = pltpu.roll(jnp.broadcast_to(e, (mq, 2 * near)), 0, 1, stride=1, stride_axis=0)
        far = jnp.broadcast_to(e[:, 0:1], (mq, win - near))
        bias = jnp.concatenate([far, shifted[:, near:]], axis=1)
        qc = lax.broadcasted_iota(jnp.int32, (mq, win), 0) // CHUNK
        kc = lax.broadcasted_iota(jnp.int32, (mq, win), 1) // CHUNK - LEFT_CHUNKS
        band = (kc <= qc) & (kc >= qc - LEFT_CHUNKS)
        bias_ref[...] = jnp.where(band, bias, NEG_INF)

    kp_ref[0:pad, :] = jnp.zeros((pad, hd), BF16)
    vp_ref[0:pad, :] = jnp.zeros((pad, hd), BF16)
    kp_ref[pad:pad + seq, :] = k_ref[0, 0]
    vp_ref[pad:pad + seq, :] = v_ref[0, 0]

    def block(q0, masked):
        q = q_ref[0, 0, pl.ds(q0, mq), :]
        kw = kp_ref[pl.ds(q0, win), :]
        vw = vp_ref[pl.ds(q0, win), :]
        s = lax.dot_general(q, kw, (((1,), (1,)), ((), ())), preferred_element_type=F32)
        s = s * scale + bias_ref[...]
        if masked:
            col = lax.broadcasted_iota(jnp.int32, (mq, win), 1)
            s = jnp.where(col >= pad - q0, s, NEG_INF)
        m = jnp.max(s, axis=-1, keepdims=True)
        p = jnp.exp(s - m)
        l = jnp.sum(p, axis=-1, keepdims=True)
        o = _dot(p.astype(BF16), vw) / l
        o_ref[0, 0, pl.ds(q0, mq), :] = o

    nblk = seq // mq
    n_static = min(-(-(pad // mq) // unroll) * unroll, nblk)
    for u in range(n_static):
        block(u * mq, masked=u * mq < pad)

    def group(gi, _):
        for u in range(unroll):
            block(pl.multiple_of((gi * unroll + u) * mq, mq), masked=False)
        return 0

    lax.fori_loop(n_static // unroll, nblk // unroll, group, 0)


def _attn_table(rel_table, mq):
    near = MAX_REL + mq
    n_rel = rel_table.shape[-1]
    lead = jnp.broadcast_to(rel_table[..., :1], rel_table.shape[:-1] + (near,))
    body = rel_table[..., :min(near, n_rel)]
    parts = [lead, body]
    if near > n_rel:
        parts.append(jnp.broadcast_to(rel_table[..., -1:], rel_table.shape[:-1] + (near - n_rel,)))
    return jnp.concatenate(parts, axis=-1)[:, :, None, :]


def _attn(q, k, v, table, layer):
    bsz, heads, seq, hd = q.shape
    mq = min(ATT_Q, seq)
    unroll = min(ATT_UNROLL, seq // mq)
    pad = LEFT_CHUNKS * CHUNK
    blk = pl.BlockSpec((1, 1, seq, hd), lambda h, b: (b, h, 0, 0))
    return pl.pallas_call(
        functools.partial(_attn_body, seq=seq, mq=mq, unroll=unroll),
        grid=(heads, bsz),
        in_specs=[blk, blk, blk,
                  pl.BlockSpec((None, None, 1, table.shape[-1]), lambda h, b: (layer, h, 0, 0))],
        out_specs=blk,
        out_shape=jax.ShapeDtypeStruct((bsz, heads, seq, hd), F32),
        scratch_shapes=[pltpu.VMEM((pad + seq, hd), BF16), pltpu.VMEM((pad + seq, hd), BF16),
                        pltpu.VMEM((mq, pad + mq), F32)],
        compiler_params=_params("parallel", "arbitrary"),
        name="attn",
    )(q, k, v, table)


def _outproj_body(yl_ref, ya_ref, gl_ref, ga_ref, w_ref, x_ref, o_ref):
    half = yl_ref.shape[1]
    yl = _rms(yl_ref[...], gl_ref[...]).astype(BF16)
    att = jnp.concatenate([ya_ref[0, hd] for hd in range(ATT_HEADS)], axis=-1)
    ya = _rms(att, ga_ref[...]).astype(BF16)
    y = _dot(yl, w_ref[0:half, :]) + _dot(ya, w_ref[half:2 * half, :])
    o_ref[...] = x_ref[...] + y


def _out_proj(y_lru, y_att, g_lru, g_att, w, x, layer):
    t, d = x.shape
    half = y_lru.shape[1]
    bsz, heads, seq, hd = y_att.shape
    tm = min(ROW_TILE, seq)
    nsb = seq // tm
    return pl.pallas_call(
        _outproj_body,
        grid=(t // tm,),
        in_specs=[
            pl.BlockSpec((tm, half), lambda i: (i, 0)),
            pl.BlockSpec((1, heads, tm, hd), lambda i: (i // nsb, 0, i % nsb, 0)),
            _layer_vec(half, layer),
            _layer_vec(half, layer),
            pl.BlockSpec((None, 2 * half, d), lambda i: (layer, 0, 0),
                         pipeline_mode=pl.Buffered(1)),
            pl.BlockSpec((tm, d), lambda i: (i, 0)),
        ],
        out_specs=pl.BlockSpec((tm, d), lambda i: (i, 0)),
        out_shape=jax.ShapeDtypeStruct((t, d), F32),
        compiler_params=_params("parallel"),
        name="out_proj",
    )(y_lru, y_att, g_lru, g_att, w, x)


def kernel(x, ffn1_norm, ffn1_w_gate, ffn1_w_up, ffn1_w_down, mix_norm, w_in, conv_w, conv_b, lru_gate_a_w, lru_gate_a_b, lru_gate_x_w, lru_gate_x_b, lru_lambda, rel_bias, lru_out_norm, att_out_norm, w_out, ffn2_norm, ffn2_w_gate, ffn2_w_up, ffn2_w_down, final_norm):
    bsz, seq, d = x.shape
    depth = w_in.shape[0]
    xf = x.reshape(bsz * seq, d)
    bf = lambda w: w.astype(BF16)
    vec = lambda p: p[:, None, :]
    wg1, wu1, wd1 = bf(ffn1_w_gate), bf(ffn1_w_up), bf(ffn1_w_down)
    wg2, wu2, wd2 = bf(ffn2_w_gate), bf(ffn2_w_up), bf(ffn2_w_down)
    w_in_b, w_out_b = bf(w_in), bf(w_out)
    wa, wx = _block_diag(lru_gate_a_w), _block_diag(lru_gate_x_w)
    table = _attn_table(rel_bias, min(ATT_Q, seq))
    n1, nm, n2 = vec(ffn1_norm), vec(mix_norm), vec(ffn2_norm)
    cb, ba, bx, lam = vec(conv_b), vec(lru_gate_a_b), vec(lru_gate_x_b), vec(lru_lambda)
    g_lru, g_att = vec(lru_out_norm), vec(att_out_norm)
    for l in range(depth):
        xf = _ffn(xf, n1, wg1, wu1, wd1, l)
        xl, gl, q, k, v = _in_proj(xf, nm, w_in_b, l, bsz, seq)
        y_lru = _lru(xl, gl, conv_w, cb, wa, ba, wx, bx, lam, l, bsz, seq)
        y_att = _attn(q, k, v, table, l)
        xf = _out_proj(y_lru, y_att, g_lru, g_att, w_out_b, xf, l)
        last = l == depth - 1
        xf = _ffn(xf, n2, wg2, wu2, wd2, l, final_gain=final_norm if last else None)
    return xf.reshape(bsz, seq, d)
```

```python
import functools

import jax
import jax.numpy as jnp
from jax import lax
from jax.experimental import pallas as pl
from jax.experimental.pallas import tpu as pltpu

F32 = jnp.float32
BF16 = jnp.bfloat16

CHUNK = 64
LEFT_CHUNKS = 8
LRU_HEADS = 16
CONV_WIDTH = 4
LRU_C = 8.0
ATT_HEADS = 8
MAX_REL = 128
EPS = 1e-6
NEG_INF = -1e30
LOG2_E = 1.4426950408889634

LANES = 128
SUBLANES = 8
MXU_DIM = 256
VMEM_LIMIT = 56 * 1024 * 1024

ROW_TILE = 512
FF_ROW_TILE = 1024
FF_TILE = 512
FF_SPLIT = 2
LRU_GROUP = MXU_DIM
LRU_ROWS = 256
ATT_Q = 128
ATT_HEADS_PER_STEP = 2


def _params(*sem, flags=None):
    return pltpu.CompilerParams(dimension_semantics=sem, vmem_limit_bytes=VMEM_LIMIT, flags=flags)


def _rms(x, g):
    ms = jnp.mean(x * x, axis=-1, keepdims=True)
    return x * lax.rsqrt(ms + EPS) * g


def _dot(a, b):
    return jnp.dot(a, b, preferred_element_type=F32)


def _layer_vec(width, layer):
    return pl.BlockSpec((None, 1, width), lambda *_: (layer, 0, 0))


def _ffn_body(*refs, n_ff, final):
    if final:
        x_ref, g_ref, wg_ref, wu_ref, wd_ref, fg_ref, o_ref, h_ref = refs
    else:
        x_ref, g_ref, wg_ref, wu_ref, wd_ref, o_ref, h_ref = refs
    j = pl.program_id(1)

    @pl.when(j == 0)
    def _():
        x = x_ref[...]
        h_ref[...] = _rms(x, g_ref[...]).astype(BF16)
        o_ref[...] = x

    h = h_ref[...]
    ts = wg_ref.shape[1] // FF_SPLIT
    for s in range(FF_SPLIT):
        g = _dot(h, wg_ref[:, s * ts:(s + 1) * ts])
        u = _dot(h, wu_ref[:, s * ts:(s + 1) * ts])
        a = (g * jax.nn.sigmoid(g) * (0.5 * u)).astype(BF16)
        o_ref[...] += _dot(a, wd_ref[s * ts:(s + 1) * ts, :])

    if final:
        @pl.when(j == n_ff - 1)
        def _():
            o_ref[...] = _rms(o_ref[...], fg_ref[...])


def _ffn(x, gain, wg, wu, wd, layer, final_gain=None):
    t, d = x.shape
    d_ff = wg.shape[2]
    tm = min(FF_ROW_TILE, t)
    tf = min(FF_TILE, d_ff)
    n_ff = d_ff // tf
    final = final_gain is not None
    in_specs = [
        pl.BlockSpec((tm, d), lambda i, j: (i, 0)),
        _layer_vec(d, layer),
        pl.BlockSpec((None, d, tf), lambda i, j: (layer, 0, j)),
        pl.BlockSpec((None, d, tf), lambda i, j: (layer, 0, j)),
        pl.BlockSpec((None, tf, d), lambda i, j: (layer, j, 0)),
    ]
    args = [x, gain, wg, wu, wd]
    if final:
        in_specs.append(pl.BlockSpec((1, d), lambda i, j: (0, 0)))
        args.append(final_gain.reshape(1, d))
    return pl.pallas_call(
        functools.partial(_ffn_body, n_ff=n_ff, final=final),
        grid=(t // tm, n_ff),
        in_specs=in_specs,
        out_specs=pl.BlockSpec((tm, d), lambda i, j: (i, 0)),
        out_shape=jax.ShapeDtypeStruct((t, d), F32),
        scratch_shapes=[pltpu.VMEM((tm, d), BF16)],
        compiler_params=_params("parallel", "arbitrary"),
        name="ffn_final" if final else "ffn",
    )(*args)


def _inproj_body(x_ref, g_ref, w_ref, xl_ref, gl_ref, q_ref, k_ref, v_ref, h_ref):
    width = xl_ref.shape[1]
    h_ref[...] = _rms(x_ref[...], g_ref[...]).astype(BF16)

    def proj(idx):
        return _dot(h_ref[...], w_ref[:, idx * width:(idx + 1) * width])

    xl_ref[...] = proj(0)
    gl_ref[...] = proj(1)
    for idx, ref in ((2, q_ref), (3, k_ref), (4, v_ref)):
        z = proj(idx).astype(BF16)
        for hd in range(ATT_HEADS):
            ref[0, hd] = z[:, hd * LANES:(hd + 1) * LANES]


def _in_proj(x, gain, w, layer, bsz, seq):
    t, d = x.shape
    width = w.shape[2] // 5
    hd = width // ATT_HEADS
    assert hd == LANES
    tm = min(ROW_TILE, seq)
    nsb = seq // tm
    row = pl.BlockSpec((tm, width), lambda i: (i, 0))
    head = pl.BlockSpec((1, ATT_HEADS, tm, hd), lambda i: (i // nsb, 0, i % nsb, 0))
    flat = jax.ShapeDtypeStruct((t, width), F32)
    heads = jax.ShapeDtypeStruct((bsz, ATT_HEADS, seq, hd), BF16)
    return pl.pallas_call(
        _inproj_body,
        grid=(t // tm,),
        in_specs=[
            pl.BlockSpec((tm, d), lambda i: (i, 0)),
            _layer_vec(d, layer),
            pl.BlockSpec((None, d, 5 * width), lambda i: (layer, 0, 0),
                         pipeline_mode=pl.Buffered(1)),
        ],
        out_specs=[row, row, head, head, head],
        out_shape=[flat, flat, heads, heads, heads],
        scratch_shapes=[pltpu.VMEM((tm, d), BF16)],
        compiler_params=_params("parallel"),
        name="in_proj",
    )(x, gain, w)


def _lru_body(xl_ref, gl_ref, cw_ref, cb_ref, wa_ref, ba_ref, wx_ref, bx_ref, lam_ref,
              y_ref, xs_ref, *, seq, rows):
    c = xl_ref.shape[1]
    tiles = rows // SUBLANES
    cw = cw_ref[...]
    cb = cb_ref[...]
    ba = ba_ref[...]
    bx = bx_ref[...]
    neg_c_sp = -LRU_C * jax.nn.softplus(-lam_ref[...])
    sub = lax.broadcasted_iota(jnp.int32, (1, SUBLANES, c), 1)

    row8 = lax.broadcasted_iota(jnp.int32, (SUBLANES, c), 0)
    for sh in range(1, CONV_WIDTH):
        first = pltpu.roll(xl_ref[0:SUBLANES, :], sh, 0)
        xs_ref[sh - 1, 0:SUBLANES, :] = jnp.where(row8 >= sh, first, 0.0)
        xs_ref[sh - 1, SUBLANES:seq, :] = xl_ref[SUBLANES - sh:seq - sh, :]

    def chunk(ci, h_prev):
        r0 = pl.multiple_of(ci * rows, rows)

        xc = cb
        for k in range(CONV_WIDTH):
            shift = CONV_WIDTH - 1 - k
            tap = xl_ref[pl.ds(r0, rows), :] if shift == 0 else xs_ref[shift - 1, pl.ds(r0, rows), :]
            xc = xc + tap * cw[k:k + 1]

        xb = xc.astype(BF16)
        r = jax.nn.sigmoid(_dot(xb, wa_ref[...]) + ba)
        gi = jax.nn.sigmoid(_dot(xb, wx_ref[...]) + bx)
        log_a = neg_c_sp * r
        a = jnp.exp(log_a)
        mult = jnp.sqrt(-jnp.tanh(log_a) * (1.0 + a * a))
        u = mult * gi * xc

        a = a.reshape(tiles, SUBLANES, c)
        u = u.reshape(tiles, SUBLANES, c)
        for k in (1, 2, 4):
            a_s = jnp.where(sub >= k, pltpu.roll(a, k, 1), 1.0)
            u_s = jnp.where(sub >= k, pltpu.roll(u, k, 1), 0.0)
            u = a * u_s + u
            a = a * a_s

        hs = []
        h = h_prev
        for t in range(tiles):
            ht = a[t] * h + u[t]
            hs.append(ht)
            h = jnp.broadcast_to(ht[SUBLANES - 1:SUBLANES], (SUBLANES, c))
        hh = jnp.concatenate(hs, axis=0)

        y_ref[pl.ds(r0, rows), :] = hh * jax.nn.gelu(gl_ref[pl.ds(r0, rows), :])
        return h

    lax.fori_loop(0, seq // rows, chunk, jnp.zeros((SUBLANES, c), F32))


def _lru(xl, gl, conv_w, conv_b, wa, ba, wx, bx, lam, layer, bsz, seq):
    t, width = xl.shape
    c = LRU_GROUP
    ng = width // c
    rows = min(LRU_ROWS, seq)
    blk = pl.BlockSpec((seq, c), lambda b, g: (b, g))
    vec = pl.BlockSpec((None, 1, c), lambda b, g: (layer, 0, g))
    mat = pl.BlockSpec((None, None, c, c), lambda b, g: (layer, g, 0, 0))
    return pl.pallas_call(
        functools.partial(_lru_body, seq=seq, rows=rows),
        grid=(bsz, ng),
        in_specs=[blk, blk, pl.BlockSpec((None, CONV_WIDTH, c), lambda b, g: (layer, 0, g)),
                  vec, mat, vec, mat, vec, vec],
        out_specs=blk,
        out_shape=jax.ShapeDtypeStruct((t, width), F32),
        scratch_shapes=[pltpu.VMEM((CONV_WIDTH - 1, seq, c), F32)],
        compiler_params=_params("parallel", "parallel"),
        name="lru",
    )(xl, gl, conv_w, conv_b, wa, ba, wx, bx, lam)


def _block_diag(w):
    depth, heads, n, _ = w.shape
    per = LRU_GROUP // n
    w5 = w.reshape(depth, heads // per, per, n, n)
    eye = jnp.eye(per, dtype=w.dtype)
    bd = w5[:, :, :, :, None, :] * eye[None, None, :, None, :, None]
    return bd.reshape(depth, heads // per, LRU_GROUP, LRU_GROUP).astype(BF16)


def _attn_body(q_ref, k_ref, v_ref, e_ref, o_ref, bias_ref, s_ref, p_ref, l_ref, *, seq, mq):
    pad = LEFT_CHUNKS * CHUNK
    win = pad + mq
    near = MAX_REL + mq
    nh, hd = q_ref.shape[1], q_ref.shape[-1]
    scale = hd ** -0.5
    nblk = seq // mq

    @pl.when(pl.program_id(1) == 0)
    def _():
        qc = lax.broadcasted_iota(jnp.int32, (mq, win), 0) // CHUNK
        kc = lax.broadcasted_iota(jnp.int32, (mq, win), 1) // CHUNK - LEFT_CHUNKS
        band = (kc <= qc) & (kc >= qc - LEFT_CHUNKS)
        for h in range(nh):
            e = e_ref[h] * (1.0 / scale)
            shifted = pltpu.roll(jnp.broadcast_to(e, (mq, 2 * near)), 0, 1,
                                 stride=1, stride_axis=0)
            far = jnp.broadcast_to(e[:, 0:1], (mq, win - near))
            bias = jnp.concatenate([far, shifted[:, near:]], axis=1)
            bias_ref[h] = jnp.where(band, bias, NEG_INF)

    def span(k):
        hi = (k + 1) * mq
        lo = max(hi - win, 0)
        return lo, hi

    def scores(h, k):
        lo, hi = span(k)
        q = q_ref[0, h, k * mq:(k + 1) * mq, :]
        s_ref[h, k % 2, :, 0:hi - lo] = lax.dot_general(
            q, k_ref[0, h, lo:hi, :], (((1,), (1,)), ((), ())), preferred_element_type=F32)

    def softmax(h, k):
        lo, hi = span(k)
        t = s_ref[h, k % 2, :, 0:hi - lo] + bias_ref[h, :, win - (hi - lo):win]
        m = jnp.max(t, axis=-1, keepdims=True)
        p = jnp.exp2((t - m) * (scale * LOG2_E))
        p_ref[h, k % 2, :, 0:hi - lo] = p.astype(BF16)
        l_ref[h, k % 2] = jnp.broadcast_to(jnp.sum(p, axis=-1, keepdims=True), (mq, hd))

    def values(h, k):
        lo, hi = span(k)
        pv = _dot(p_ref[h, k % 2, :, 0:hi - lo], v_ref[0, h, lo:hi, :])
        o_ref[0, h, k * mq:(k + 1) * mq, :] = pv / l_ref[h, k % 2]

    for k in range(nblk + 2):
        for h in range(nh):
            if k < nblk:
                scores(h, k)
            if 1 <= k <= nblk:
                softmax(h, k - 1)
            if k >= 2:
                values(h, k - 2)


def _attn_table(rel_table, mq):
    near = MAX_REL + mq
    n_rel = rel_table.shape[-1]
    lead = jnp.broadcast_to(rel_table[..., :1], rel_table.shape[:-1] + (near,))
    body = rel_table[..., :min(near, n_rel)]
    parts = [lead, body]
    if near > n_rel:
        parts.append(jnp.broadcast_to(rel_table[..., -1:], rel_table.shape[:-1] + (near - n_rel,)))
    return jnp.concatenate(parts, axis=-1)[:, :, None, :]


def _attn(q, k, v, table, layer):
    bsz, heads, seq, hd = q.shape
    mq = min(ATT_Q, seq)
    pad = LEFT_CHUNKS * CHUNK
    win = pad + mq
    nh = ATT_HEADS_PER_STEP
    blk = pl.BlockSpec((1, nh, seq, hd), lambda h, b: (b, h, 0, 0))
    return pl.pallas_call(
        functools.partial(_attn_body, seq=seq, mq=mq),
        grid=(heads // nh, bsz),
        in_specs=[blk, blk, blk,
                  pl.BlockSpec((None, nh, 1, table.shape[-1]), lambda h, b: (layer, h, 0, 0))],
        out_specs=blk,
        out_shape=jax.ShapeDtypeStruct((bsz, heads, seq, hd), F32),
        scratch_shapes=[pltpu.VMEM((nh, mq, win), F32), pltpu.VMEM((nh, 2, mq, win), F32),
                        pltpu.VMEM((nh, 2, mq, win), BF16), pltpu.VMEM((nh, 2, mq, hd), F32)],
        compiler_params=_params("parallel", "arbitrary"),
        name="attn",
    )(q, k, v, table)


def _outproj_body(yl_ref, ya_ref, gl_ref, ga_ref, w_ref, x_ref, o_ref):
    half = yl_ref.shape[1]
    yl = _rms(yl_ref[...], gl_ref[...]).astype(BF16)
    att = jnp.concatenate([ya_ref[0, hd] for hd in range(ATT_HEADS)], axis=-1)
    ya = _rms(att, ga_ref[...]).astype(BF16)
    y = _dot(yl, w_ref[0:half, :]) + _dot(ya, w_ref[half:2 * half, :])
    o_ref[...] = x_ref[...] + y


def _out_proj(y_lru, y_att, g_lru, g_att, w, x, layer):
    t, d = x.shape
    half = y_lru.shape[1]
    bsz, heads, seq, hd = y_att.shape
    tm = min(ROW_TILE, seq)
    nsb = seq // tm
    return pl.pallas_call(
        _outproj_body,
        grid=(t // tm,),
        in_specs=[
            pl.BlockSpec((tm, half), lambda i: (i, 0)),
            pl.BlockSpec((1, heads, tm, hd), lambda i: (i // nsb, 0, i % nsb, 0)),
            _layer_vec(half, layer),
            _layer_vec(half, layer),
            pl.BlockSpec((None, 2 * half, d), lambda i: (layer, 0, 0),
                         pipeline_mode=pl.Buffered(1)),
            pl.BlockSpec((tm, d), lambda i: (i, 0)),
        ],
        out_specs=pl.BlockSpec((tm, d), lambda i: (i, 0)),
        out_shape=jax.ShapeDtypeStruct((t, d), F32),
        compiler_params=_params("parallel"),
        name="out_proj",
    )(y_lru, y_att, g_lru, g_att, w, x)


def kernel(x, ffn1_norm, ffn1_w_gate, ffn1_w_up, ffn1_w_down, mix_norm, w_in, conv_w, conv_b, lru_gate_a_w, lru_gate_a_b, lru_gate_x_w, lru_gate_x_b, lru_lambda, rel_bias, lru_out_norm, att_out_norm, w_out, ffn2_norm, ffn2_w_gate, ffn2_w_up, ffn2_w_down, final_norm):
    bsz, seq, d = x.shape
    depth = w_in.shape[0]
    xf = x.reshape(bsz * seq, d)
    bf = lambda w: w.astype(BF16)
    vec = lambda p: p[:, None, :]
    wg1, wu1, wd1 = bf(ffn1_w_gate), bf(ffn1_w_up), bf(ffn1_w_down)
    wg2, wu2, wd2 = bf(ffn2_w_gate), bf(ffn2_w_up), bf(ffn2_w_down)
    w_in_b, w_out_b = bf(w_in), bf(w_out)
    wa, wx = _block_diag(lru_gate_a_w), _block_diag(lru_gate_x_w)
    table = _attn_table(rel_bias, min(ATT_Q, seq))
    n1, nm, n2 = vec(ffn1_norm), vec(mix_norm), vec(ffn2_norm)
    cb, ba, bx, lam = vec(conv_b), vec(lru_gate_a_b), vec(lru_gate_x_b), vec(lru_lambda)
    g_lru, g_att = vec(lru_out_norm), vec(att_out_norm)
    for l in range(depth):
        xf = _ffn(xf, n1, wg1, wu1, wd1, l)
        xl, gl, q, k, v = _in_proj(xf, nm, w_in_b, l, bsz, seq)
        y_lru = _lru(xl, gl, conv_w, cb, wa, ba, wx, bx, lam, l, bsz, seq)
        y_att = _attn(q, k, v, table, l)
        xf = _out_proj(y_lru, y_att, g_lru, g_att, w_out_b, xf, l)
        last = l == depth - 1
        xf = _ffn(xf, n2, wg2, wu2, wd2, l, final_gain=final_norm if last else None)
    return xf.reshape(bsz, seq, d)
```

```python
import functools

import jax
import jax.numpy as jnp
from jax import lax
from jax.experimental import pallas as pl
from jax.experimental.pallas import tpu as pltpu

F32 = jnp.float32
BF16 = jnp.bfloat16

CHUNK = 64
LEFT_CHUNKS = 8
LRU_HEADS = 16
CONV_WIDTH = 4
LRU_C = 8.0
ATT_HEADS = 8
MAX_REL = 128
EPS = 1e-6
NEG_INF = -1e30
LOG2_E = 1.4426950408889634

LANES = 128
SUBLANES = 8
MXU_DIM = 256
VMEM_LIMIT = 56 * 1024 * 1024

ROW_TILE = 512
FF_ROW_TILE = 1024
FF_TILE = 512
FF_SPLIT = 2
LRU_GROUP = MXU_DIM
LRU_ROWS = 256
ATT_Q = 128
ATT_HEADS_PER_STEP = 2


def _params(*sem, flags=None):
    return pltpu.CompilerParams(dimension_semantics=sem, vmem_limit_bytes=VMEM_LIMIT, flags=flags)


def _rms(x, g):
    ms = jnp.mean(x * x, axis=-1, keepdims=True)
    return x * lax.rsqrt(ms + EPS) * g


def _dot(a, b):
    return jnp.dot(a, b, preferred_element_type=F32)


def _layer_vec(width, layer):
    return pl.BlockSpec((None, 1, width), lambda *_: (layer, 0, 0))


def _ffn_body(*refs, n_ff, final):
    if final:
        x_ref, g_ref, wg_ref, wu_ref, wd_ref, fg_ref, o_ref, h_ref = refs
    else:
        x_ref, g_ref, wg_ref, wu_ref, wd_ref, o_ref, h_ref = refs
    j = pl.program_id(1)

    def step(first):
        if first:
            h = _rms(x_ref[...], g_ref[...]).astype(BF16)
            h_ref[...] = h
        else:
            h = h_ref[...]
        ts = wg_ref.shape[1] // FF_SPLIT
        for s in range(FF_SPLIT):
            g = _dot(h, wg_ref[:, s * ts:(s + 1) * ts])
            u = _dot(h, wu_ref[:, s * ts:(s + 1) * ts])
            a = (g * jax.nn.sigmoid(g) * (0.5 * u)).astype(BF16)
            d = _dot(a, wd_ref[s * ts:(s + 1) * ts, :])
            if first and s == 0:
                o_ref[...] = x_ref[...] + d
            else:
                o_ref[...] += d

    pl.when(j == 0)(functools.partial(step, True))
    pl.when(j > 0)(functools.partial(step, False))

    if final:
        @pl.when(j == n_ff - 1)
        def _():
            o_ref[...] = _rms(o_ref[...], fg_ref[...])


def _ffn(x, gain, wg, wu, wd, layer, final_gain=None):
    t, d = x.shape
    d_ff = wg.shape[2]
    tm = min(FF_ROW_TILE, t)
    tf = min(FF_TILE, d_ff)
    n_ff = d_ff // tf
    final = final_gain is not None
    in_specs = [
        pl.BlockSpec((tm, d), lambda i, j: (i, 0)),
        _layer_vec(d, layer),
        pl.BlockSpec((None, d, tf), lambda i, j: (layer, 0, j)),
        pl.BlockSpec((None, d, tf), lambda i, j: (layer, 0, j)),
        pl.BlockSpec((None, tf, d), lambda i, j: (layer, j, 0)),
    ]
    args = [x, gain, wg, wu, wd]
    if final:
        in_specs.append(pl.BlockSpec((1, d), lambda i, j: (0, 0)))
        args.append(final_gain.reshape(1, d))
    return pl.pallas_call(
        functools.partial(_ffn_body, n_ff=n_ff, final=final),
        grid=(t // tm, n_ff),
        in_specs=in_specs,
        out_specs=pl.BlockSpec((tm, d), lambda i, j: (i, 0)),
        out_shape=jax.ShapeDtypeStruct((t, d), F32),
        scratch_shapes=[pltpu.VMEM((tm, d), BF16)],
        compiler_params=_params("parallel", "arbitrary"),
        name="ffn_final" if final else "ffn",
    )(*args)


def _inproj_body(x_ref, g_ref, w_ref, xl_ref, gl_ref, q_ref, k_ref, v_ref, h_ref):
    width = xl_ref.shape[1]
    h_ref[...] = _rms(x_ref[...], g_ref[...]).astype(BF16)

    def proj(idx):
        return _dot(h_ref[...], w_ref[:, idx * width:(idx + 1) * width])

    xl_ref[...] = proj(0)
    gl_ref[...] = proj(1)
    for idx, ref in ((2, q_ref), (3, k_ref), (4, v_ref)):
        z = proj(idx).astype(BF16)
        for hd in range(ATT_HEADS):
            ref[0, hd] = z[:, hd * LANES:(hd + 1) * LANES]


def _in_proj(x, gain, w, layer, bsz, seq):
    t, d = x.shape
    width = w.shape[2] // 5
    hd = width // ATT_HEADS
    assert hd == LANES
    tm = min(ROW_TILE, seq)
    nsb = seq // tm
    row = pl.BlockSpec((tm, width), lambda i: (i, 0))
    head = pl.BlockSpec((1, ATT_HEADS, tm, hd), lambda i: (i // nsb, 0, i % nsb, 0))
    flat = jax.ShapeDtypeStruct((t, width), F32)
    heads = jax.ShapeDtypeStruct((bsz, ATT_HEADS, seq, hd), BF16)
    return pl.pallas_call(
        _inproj_body,
        grid=(t // tm,),
        in_specs=[
            pl.BlockSpec((tm, d), lambda i: (i, 0)),
            _layer_vec(d, layer),
            pl.BlockSpec((None, d, 5 * width), lambda i: (layer, 0, 0),
                         pipeline_mode=pl.Buffered(1)),
        ],
        out_specs=[row, row, head, head, head],
        out_shape=[flat, flat, heads, heads, heads],
        scratch_shapes=[pltpu.VMEM((tm, d), BF16)],
        compiler_params=_params("parallel"),
        name="in_proj",
    )(x, gain, w)


def _lru_body(xl_ref, gl_ref, cw_ref, cb_ref, wa_ref, ba_ref, wx_ref, bx_ref, lam_ref,
              y_ref, xs_ref, *, seq, rows):
    c = xl_ref.shape[1]
    tiles = rows // SUBLANES
    cw = cw_ref[...]
    cb = cb_ref[...]
    ba = ba_ref[...]
    bx = bx_ref[...]
    neg_c_sp = -LRU_C * jax.nn.softplus(-lam_ref[...])
    sub = lax.broadcasted_iota(jnp.int32, (1, SUBLANES, c), 1)

    row8 = lax.broadcasted_iota(jnp.int32, (SUBLANES, c), 0)
    for sh in range(1, CONV_WIDTH):
        first = pltpu.roll(xl_ref[0:SUBLANES, :], sh, 0)
        xs_ref[sh - 1, 0:SUBLANES, :] = jnp.where(row8 >= sh, first, 0.0)
        xs_ref[sh - 1, SUBLANES:seq, :] = xl_ref[SUBLANES - sh:seq - sh, :]

    def chunk(ci, h_prev):
        r0 = pl.multiple_of(ci * rows, rows)

        xc = cb
        for k in range(CONV_WIDTH):
            shift = CONV_WIDTH - 1 - k
            tap = xl_ref[pl.ds(r0, rows), :] if shift == 0 else xs_ref[shift - 1, pl.ds(r0, rows), :]
            xc = xc + tap * cw[k:k + 1]

        xb = xc.astype(BF16)
        r = jax.nn.sigmoid(_dot(xb, wa_ref[...]) + ba)
        gi = jax.nn.sigmoid(_dot(xb, wx_ref[...]) + bx)
        log_a = neg_c_sp * r
        a = jnp.exp(log_a)
        mult = jnp.sqrt(-jnp.tanh(log_a) * (1.0 + a * a))
        u = mult * gi * xc

        a = a.reshape(tiles, SUBLANES, c)
        u = u.reshape(tiles, SUBLANES, c)
        for k in (1, 2, 4):
            a_s = jnp.where(sub >= k, pltpu.roll(a, k, 1), 1.0)
            u_s = jnp.where(sub >= k, pltpu.roll(u, k, 1), 0.0)
            u = a * u_s + u
            a = a * a_s

        hs = []
        h = h_prev
        for t in range(tiles):
            ht = a[t] * h + u[t]
            hs.append(ht)
            h = jnp.broadcast_to(ht[SUBLANES - 1:SUBLANES], (SUBLANES, c))
        hh = jnp.concatenate(hs, axis=0)

        y_ref[pl.ds(r0, rows), :] = hh * jax.nn.gelu(gl_ref[pl.ds(r0, rows), :])
        return h

    lax.fori_loop(0, seq // rows, chunk, jnp.zeros((SUBLANES, c), F32))


def _lru(xl, gl, conv_w, conv_b, wa, ba, wx, bx, lam, layer, bsz, seq):
    t, width = xl.shape
    c = LRU_GROUP
    ng = width // c
    rows = min(LRU_ROWS, seq)
    blk = pl.BlockSpec((seq, c), lambda b, g: (b, g))
    vec = pl.BlockSpec((None, 1, c), lambda b, g: (layer, 0, g))
    mat = pl.BlockSpec((None, None, c, c), lambda b, g: (layer, g, 0, 0))
    return pl.pallas_call(
        functools.partial(_lru_body, seq=seq, rows=rows),
        grid=(bsz, ng),
        in_specs=[blk, blk, pl.BlockSpec((None, CONV_WIDTH, c), lambda b, g: (layer, 0, g)),
                  vec, mat, vec, mat, vec, vec],
        out_specs=blk,
        out_shape=jax.ShapeDtypeStruct((t, width), F32),
        scratch_shapes=[pltpu.VMEM((CONV_WIDTH - 1, seq, c), F32)],
        compiler_params=_params("parallel", "parallel"),
        name="lru",
    )(xl, gl, conv_w, conv_b, wa, ba, wx, bx, lam)


def _block_diag(w):
    depth, heads, n, _ = w.shape
    per = LRU_GROUP // n
    w5 = w.reshape(depth, heads // per, per, n, n)
    eye = jnp.eye(per, dtype=w.dtype)
    bd = w5[:, :, :, :, None, :] * eye[None, None, :, None, :, None]
    return bd.reshape(depth, heads // per, LRU_GROUP, LRU_GROUP).astype(BF16)


def _attn_body(q_ref, k_ref, v_ref, e_ref, o_ref, bias_ref, s_ref, p_ref, l_ref, *, seq, mq):
    pad = LEFT_CHUNKS * CHUNK
    win = pad + mq
    near = MAX_REL + mq
    nh, hd = q_ref.shape[1], q_ref.shape[-1]
    scale = hd ** -0.5
    nblk = seq // mq

    @pl.when(pl.program_id(1) == 0)
    def _():
        qc = lax.broadcasted_iota(jnp.int32, (mq, win), 0) // CHUNK
        kc = lax.broadcasted_iota(jnp.int32, (mq, win), 1) // CHUNK - LEFT_CHUNKS
        band = (kc <= qc) & (kc >= qc - LEFT_CHUNKS)
        for h in range(nh):
            e = e_ref[h] * (1.0 / scale)
            shifted = pltpu.roll(jnp.broadcast_to(e, (mq, 2 * near)), 0, 1,
                                 stride=1, stride_axis=0)
            far = jnp.broadcast_to(e[:, 0:1], (mq, win - near))
            bias = jnp.concatenate([far, shifted[:, near:]], axis=1)
            bias_ref[h] = jnp.where(band, bias, NEG_INF)

    def span(k):
        hi = (k + 1) * mq
        lo = max(hi - win, 0)
        return lo, hi

    def scores(h, k):
        lo, hi = span(k)
        q = q_ref[0, h, k * mq:(k + 1) * mq, :]
        s_ref[h, k % 2, :, 0:hi - lo] = lax.dot_general(
            q, k_ref[0, h, lo:hi, :], (((1,), (1,)), ((), ())), preferred_element_type=F32)

    def softmax(h, k):
        lo, hi = span(k)
        t = s_ref[h, k % 2, :, 0:hi - lo] + bias_ref[h, :, win - (hi - lo):win]
        m = jnp.max(t, axis=-1, keepdims=True)
        p = jnp.exp2((t - m) * (scale * LOG2_E))
        p_ref[h, k % 2, :, 0:hi - lo] = p.astype(BF16)
        l_ref[h, k % 2] = jnp.broadcast_to(jnp.sum(p, axis=-1, keepdims=True), (mq, hd))

    def values(h, k):
        lo, hi = span(k)
        pv = _dot(p_ref[h, k % 2, :, 0:hi - lo], v_ref[0, h, lo:hi, :])
        o_ref[0, h, k * mq:(k + 1) * mq, :] = pv / l_ref[h, k % 2]

    for k in range(nblk + 2):
        for h in range(nh):
            if k < nblk:
                scores(h, k)
            if 1 <= k <= nblk:
                softmax(h, k - 1)
            if k >= 2:
                values(h, k - 2)


def _attn_table(rel_table, mq):
    near = MAX_REL + mq
    n_rel = rel_table.shape[-1]
    lead = jnp.broadcast_to(rel_table[..., :1], rel_table.shape[:-1] + (near,))
    body = rel_table[..., :min(near, n_rel)]
    parts = [lead, body]
    if near > n_rel:
        parts.append(jnp.broadcast_to(rel_table[..., -1:], rel_table.shape[:-1] + (near - n_rel,)))
    return jnp.concatenate(parts, axis=-1)[:, :, None, :]


def _attn(q, k, v, table, layer):
    bsz, heads, seq, hd = q.shape
    mq = min(ATT_Q, seq)
    pad = LEFT_CHUNKS * CHUNK
    win = pad + mq
    nh = ATT_HEADS_PER_STEP
    blk = pl.BlockSpec((1, nh, seq, hd), lambda h, b: (b, h, 0, 0))
    return pl.pallas_call(
        functools.partial(_attn_body, seq=seq, mq=mq),
        grid=(heads // nh, bsz),
        in_specs=[blk, blk, blk,
                  pl.BlockSpec((None, nh, 1, table.shape[-1]), lambda h, b: (layer, h, 0, 0))],
        out_specs=blk,
        out_shape=jax.ShapeDtypeStruct((bsz, heads, seq, hd), F32),
        scratch_shapes=[pltpu.VMEM((nh, mq, win), F32), pltpu.VMEM((nh, 2, mq, win), F32),
                        pltpu.VMEM((nh, 2, mq, win), BF16), pltpu.VMEM((nh, 2, mq, hd), F32)],
        compiler_params=_params("parallel", "arbitrary"),
        name="attn",
    )(q, k, v, table)


def _outproj_body(yl_ref, ya_ref, gl_ref, ga_ref, w_ref, x_ref, o_ref):
    half = yl_ref.shape[1]
    yl = _rms(yl_ref[...], gl_ref[...]).astype(BF16)
    att = jnp.concatenate([ya_ref[0, hd] for hd in range(ATT_HEADS)], axis=-1)
    ya = _rms(att, ga_ref[...]).astype(BF16)
    y = _dot(yl, w_ref[0:half, :]) + _dot(ya, w_ref[half:2 * half, :])
    o_ref[...] = x_ref[...] + y


def _out_proj(y_lru, y_att, g_lru, g_att, w, x, layer):
    t, d = x.shape
    half = y_lru.shape[1]
    bsz, heads, seq, hd = y_att.shape
    tm = min(ROW_TILE, seq)
    nsb = seq // tm
    return pl.pallas_call(
        _outproj_body,
        grid=(t // tm,),
        in_specs=[
            pl.BlockSpec((tm, half), lambda i: (i, 0)),
            pl.BlockSpec((1, heads, tm, hd), lambda i: (i // nsb, 0, i % nsb, 0)),
            _layer_vec(half, layer),
            _layer_vec(half, layer),
            pl.BlockSpec((None, 2 * half, d), lambda i: (layer, 0, 0),
                         pipeline_mode=pl.Buffered(1)),
            pl.BlockSpec((tm, d), lambda i: (i, 0)),
        ],
        out_specs=pl.BlockSpec((tm, d), lambda i: (i, 0)),
        out_shape=jax.ShapeDtypeStruct((t, d), F32),
        compiler_params=_params("parallel"),
        name="out_proj",
    )(y_lru, y_att, g_lru, g_att, w, x)


def kernel(x, ffn1_norm, ffn1_w_gate, ffn1_w_up, ffn1_w_down, mix_norm, w_in, conv_w, conv_b, lru_gate_a_w, lru_gate_a_b, lru_gate_x_w, lru_gate_x_b, lru_lambda, rel_bias, lru_out_norm, att_out_norm, w_out, ffn2_norm, ffn2_w_gate, ffn2_w_up, ffn2_w_down, final_norm):
    bsz, seq, d = x.shape
    depth = w_in.shape[0]
    xf = x.reshape(bsz * seq, d)
    bf = lambda w: w.astype(BF16)
    vec = lambda p: p[:, None, :]
    wg1, wu1, wd1 = bf(ffn1_w_gate), bf(ffn1_w_up), bf(ffn1_w_down)
    wg2, wu2, wd2 = bf(ffn2_w_gate), bf(ffn2_w_up), bf(ffn2_w_down)
    w_in_b, w_out_b = bf(w_in), bf(w_out)
    wa, wx = _block_diag(lru_gate_a_w), _block_diag(lru_gate_x_w)
    table = _attn_table(rel_bias, min(ATT_Q, seq))
    n1, nm, n2 = vec(ffn1_norm), vec(mix_norm), vec(ffn2_norm)
    cb, ba, bx, lam = vec(conv_b), vec(lru_gate_a_b), vec(lru_gate_x_b), vec(lru_lambda)
    g_lru, g_att = vec(lru_out_norm), vec(att_out_norm)
    for l in range(depth):
        xf = _ffn(xf, n1, wg1, wu1, wd1, l)
        xl, gl, q, k, v = _in_proj(xf, nm, w_in_b, l, bsz, seq)
        y_lru = _lru(xl, gl, conv_w, cb, wa, ba, wx, bx, lam, l, bsz, seq)
        y_att = _attn(q, k, v, table, l)
        xf = _out_proj(y_lru, y_att, g_lru, g_att, w_out_b, xf, l)
        last = l == depth - 1
        xf = _ffn(xf, n2, wg2, wu2, wd2, l, final_gain=final_norm if last else None)
    return xf.reshape(bsz, seq, d)
```

```python
import functools

import jax
import jax.numpy as jnp
from jax import lax
from jax.experimental import pallas as pl
from jax.experimental.pallas import tpu as pltpu

F32 = jnp.float32
BF16 = jnp.bfloat16

CHUNK = 64
LEFT_CHUNKS = 8
LRU_HEADS = 16
CONV_WIDTH = 4
N_PROJ = 5
LRU_C = 8.0
ATT_HEADS = 8
MAX_REL = 128
EPS = 1e-6
NEG_INF = -1e30
LOG2_E = 1.4426950408889634

LANES = 128
SUBLANES = 8
MXU_DIM = 256
VMEM_LIMIT = 56 * 1024 * 1024

ROW_TILE = 512
FF_ROW_TILE = 1024
FF_TILE = 512
FF_SPLIT = 2
LRU_GROUP = MXU_DIM
LRU_ROWS = 128
ATT_Q = 128
ATT_HEADS_PER_STEP = 2


def _params(*sem):
    return pltpu.CompilerParams(dimension_semantics=sem, vmem_limit_bytes=VMEM_LIMIT)


def _rms(x, g):
    ms = jnp.mean(x * x, axis=-1, keepdims=True)
    return x * lax.rsqrt(ms + EPS) * g


def _dot(a, b):
    return jnp.dot(a, b, preferred_element_type=F32)


def _layer_vec(width, layer):
    return pl.BlockSpec((None, 1, width), lambda *_: (layer, 0, 0))


def _ffn_body(*refs, n_ff, final):
    if final:
        x_ref, g_ref, wg_ref, wu_ref, wd_ref, fg_ref, o_ref, h_ref = refs
    else:
        x_ref, g_ref, wg_ref, wu_ref, wd_ref, o_ref, h_ref = refs
    j = pl.program_id(1)

    def step(first):
        if first:
            h = _rms(x_ref[...], g_ref[...]).astype(BF16)
            h_ref[...] = h
        else:
            h = h_ref[...]
        ts = wg_ref.shape[1] // FF_SPLIT
        for s in range(FF_SPLIT):
            g = _dot(h, wg_ref[:, s * ts:(s + 1) * ts])
            u = _dot(h, wu_ref[:, s * ts:(s + 1) * ts])
            a = (g * jax.nn.sigmoid(g) * (0.5 * u)).astype(BF16)
            d = _dot(a, wd_ref[s * ts:(s + 1) * ts, :])
            if first and s == 0:
                o_ref[...] = x_ref[...] + d
            else:
                o_ref[...] += d

    pl.when(j == 0)(functools.partial(step, True))
    pl.when(j > 0)(functools.partial(step, False))

    if final:
        @pl.when(j == n_ff - 1)
        def _():
            o_ref[...] = _rms(o_ref[...], fg_ref[...])


def _ffn(x, gain, wg, wu, wd, layer, final_gain=None):
    t, d = x.shape
    d_ff = wg.shape[2]
    tm = min(FF_ROW_TILE, t)
    tf = min(FF_TILE, d_ff)
    n_ff = d_ff // tf
    final = final_gain is not None
    in_specs = [
        pl.BlockSpec((tm, d), lambda i, j: (i, 0)),
        _layer_vec(d, layer),
        pl.BlockSpec((None, d, tf), lambda i, j: (layer, 0, j)),
        pl.BlockSpec((None, d, tf), lambda i, j: (layer, 0, j)),
        pl.BlockSpec((None, tf, d), lambda i, j: (layer, j, 0)),
    ]
    args = [x, gain, wg, wu, wd]
    if final:
        in_specs.append(pl.BlockSpec((1, d), lambda i, j: (0, 0)))
        args.append(final_gain.reshape(1, d))
    return pl.pallas_call(
        functools.partial(_ffn_body, n_ff=n_ff, final=final),
        grid=(t // tm, n_ff),
        in_specs=in_specs,
        out_specs=pl.BlockSpec((tm, d), lambda i, j: (i, 0)),
        out_shape=jax.ShapeDtypeStruct((t, d), F32),
        scratch_shapes=[pltpu.VMEM((tm, d), BF16)],
        compiler_params=_params("parallel", "arbitrary"),
        name="ffn_final" if final else "ffn",
    )(*args)


def _lru_rows(zz_ref, gate, cw, cb, wa, ba, wx, bx, neg_c_sp, h0, cols, r0, rows):
    c = LRU_GROUP
    tiles = rows // SUBLANES
    xc = cb
    for k in range(CONV_WIDTH):
        start = r0 + SUBLANES - (CONV_WIDTH - 1 - k)
        xc = xc + zz_ref[start:start + rows, cols] * cw[k:k + 1]

    xb = xc.astype(BF16)
    r = jax.nn.sigmoid(_dot(xb, wa) + ba)
    gi = jax.nn.sigmoid(_dot(xb, wx) + bx)
    log_a = neg_c_sp * r
    a = jnp.exp(log_a)
    mult = jnp.sqrt(-jnp.tanh(log_a) * (1.0 + a * a))
    u = mult * gi * xc

    sub = lax.broadcasted_iota(jnp.int32, (1, SUBLANES, c), 1)
    a = a.reshape(tiles, SUBLANES, c)
    u = u.reshape(tiles, SUBLANES, c)
    for k in (1, 2, 4):
        a_s = jnp.where(sub >= k, pltpu.roll(a, k, 1), 1.0)
        u_s = jnp.where(sub >= k, pltpu.roll(u, k, 1), 0.0)
        u = a * u_s + u
        a = a * a_s

    hs = []
    h = h0
    for t in range(tiles):
        ht = a[t] * h + u[t]
        hs.append(ht)
        h = jnp.broadcast_to(ht[SUBLANES - 1:SUBLANES], (SUBLANES, c))
    return jnp.concatenate(hs, axis=0) * jax.nn.gelu(gate), h


def _inproj_body(x_ref, g_ref, w_ref, cw_ref, cb_ref, wa_ref, ba_ref, wx_ref, bx_ref, lam_ref,
                 y_ref, q_ref, k_ref, v_ref, h_ref, zz_ref, gate_ref, hc_ref, *, nsb):
    tm, width = y_ref.shape
    h_ref[...] = _rms(x_ref[...], g_ref[...]).astype(BF16)

    def proj(lo, n):
        return _dot(h_ref[...], w_ref[:, lo:lo + n])

    first = pl.program_id(0) % nsb == 0
    zz_ref[0:SUBLANES, :] = jnp.where(first, 0.0, zz_ref[tm:tm + SUBLANES, :])
    neg_c_sp = -LRU_C * jax.nn.softplus(-lam_ref[...])
    ng = width // LRU_GROUP

    def project_group(g):
        cols = slice(g * LRU_GROUP, (g + 1) * LRU_GROUP)
        zz_ref[SUBLANES:SUBLANES + tm, cols] = proj(g * LRU_GROUP, LRU_GROUP)
        gate_ref[:, cols] = proj(width + g * LRU_GROUP, LRU_GROUP)

    def lru_group(g):
        cols = slice(g * LRU_GROUP, (g + 1) * LRU_GROUP)
        h = jnp.where(first, 0.0, hc_ref[:, cols])
        for r0 in range(0, tm, LRU_ROWS):
            y, h = _lru_rows(zz_ref, gate_ref[r0:r0 + LRU_ROWS, cols], cw_ref[:, cols],
                             cb_ref[:, cols], wa_ref[g], ba_ref[:, cols], wx_ref[g],
                             bx_ref[:, cols], neg_c_sp[:, cols], h, cols, r0, LRU_ROWS)
            y_ref[r0:r0 + LRU_ROWS, cols] = y
        hc_ref[:, cols] = h

    def project_heads(idx, ref):
        z = proj(idx * width, width).astype(BF16)
        for hd in range(ATT_HEADS):
            ref[0, hd] = z[:, hd * LANES:(hd + 1) * LANES]

    project_group(0)
    for g in range(ng):
        if g + 1 < ng:
            project_group(g + 1)
        else:
            project_heads(2, q_ref)
        lru_group(g)
    project_heads(3, k_ref)
    project_heads(4, v_ref)


def _in_proj(x, gain, w, conv_w, conv_b, wa, ba, wx, bx, lam, layer, bsz, seq):
    t, d = x.shape
    width = w.shape[2] // N_PROJ
    hd = width // ATT_HEADS
    assert hd == LANES
    tm = min(ROW_TILE, seq)
    nsb = seq // tm
    ng = width // LRU_GROUP
    vec = _layer_vec(width, layer)
    mat = pl.BlockSpec((None, ng, LRU_GROUP, LRU_GROUP), lambda i: (layer, 0, 0, 0))
    head = pl.BlockSpec((1, ATT_HEADS, tm, hd), lambda i: (i // nsb, 0, i % nsb, 0))
    heads = jax.ShapeDtypeStruct((bsz, ATT_HEADS, seq, hd), BF16)
    return pl.pallas_call(
        functools.partial(_inproj_body, nsb=nsb),
        grid=(t // tm,),
        in_specs=[
            pl.BlockSpec((tm, d), lambda i: (i, 0)),
            _layer_vec(d, layer),
            pl.BlockSpec((None, d, N_PROJ * width), lambda i: (layer, 0, 0),
                         pipeline_mode=pl.Buffered(1)),
            pl.BlockSpec((None, CONV_WIDTH, width), lambda i: (layer, 0, 0)),
            vec, mat, vec, mat, vec, vec,
        ],
        out_specs=[pl.BlockSpec((tm, width), lambda i: (i, 0)), head, head, head],
        out_shape=[jax.ShapeDtypeStruct((t, width), F32), heads, heads, heads],
        scratch_shapes=[pltpu.VMEM((tm, d), BF16),
                        pltpu.VMEM((tm + SUBLANES, width), F32),
                        pltpu.VMEM((tm, width), F32),
                        pltpu.VMEM((SUBLANES, width), F32)],
        compiler_params=_params("arbitrary"),
        name="in_proj",
    )(x, gain, w, conv_w, conv_b, wa, ba, wx, bx, lam)


def _block_diag(w):
    depth, heads, n, _ = w.shape
    per = LRU_GROUP // n
    w5 = w.reshape(depth, heads // per, per, n, n)
    eye = jnp.eye(per, dtype=w.dtype)
    bd = w5[:, :, :, :, None, :] * eye[None, None, :, None, :, None]
    return bd.reshape(depth, heads // per, LRU_GROUP, LRU_GROUP).astype(BF16)


def _attn_body(q_ref, k_ref, v_ref, e_ref, o_ref, bias_ref, s_ref, p_ref, l_ref, *, seq, mq):
    pad = LEFT_CHUNKS * CHUNK
    win = pad + mq
    near = MAX_REL + mq
    nh, hd = q_ref.shape[1], q_ref.shape[-1]
    scale = hd ** -0.5
    nblk = seq // mq

    @pl.when(pl.program_id(1) == 0)
    def _():
        qc = lax.broadcasted_iota(jnp.int32, (mq, win), 0) // CHUNK
        kc = lax.broadcasted_iota(jnp.int32, (mq, win), 1) // CHUNK - LEFT_CHUNKS
        band = (kc <= qc) & (kc >= qc - LEFT_CHUNKS)
        for h in range(nh):
            e = e_ref[h] * (1.0 / scale)
            shifted = pltpu.roll(jnp.broadcast_to(e, (mq, 2 * near)), 0, 1,
                                 stride=1, stride_axis=0)
            far = jnp.broadcast_to(e[:, 0:1], (mq, win - near))
            bias = jnp.concatenate([far, shifted[:, near:]], axis=1)
            bias_ref[h] = jnp.where(band, bias, NEG_INF)

    def span(k):
        hi = (k + 1) * mq
        lo = max(hi - win, 0)
        return lo, hi

    def scores(h, k):
        lo, hi = span(k)
        q = q_ref[0, h, k * mq:(k + 1) * mq, :]
        s_ref[h, k % 2, :, 0:hi - lo] = lax.dot_general(
            q, k_ref[0, h, lo:hi, :], (((1,), (1,)), ((), ())), preferred_element_type=F32)

    def softmax(h, k):
        lo, hi = span(k)
        t = s_ref[h, k % 2, :, 0:hi - lo] + bias_ref[h, :, win - (hi - lo):win]
        m = jnp.max(t, axis=-1, keepdims=True)
        p = jnp.exp2((t - m) * (scale * LOG2_E))
        p_ref[h, k % 2, :, 0:hi - lo] = p.astype(BF16)
        l_ref[h, k % 2] = jnp.broadcast_to(jnp.sum(p, axis=-1, keepdims=True), (mq, hd))

    def values(h, k):
        lo, hi = span(k)
        pv = _dot(p_ref[h, k % 2, :, 0:hi - lo], v_ref[0, h, lo:hi, :])
        o_ref[0, h, k * mq:(k + 1) * mq, :] = pv / l_ref[h, k % 2]

    for k in range(nblk + 2):
        for h in range(nh):
            if k < nblk:
                scores(h, k)
            if 1 <= k <= nblk:
                softmax(h, k - 1)
            if k >= 2:
                values(h, k - 2)


def _attn_table(rel_table, mq):
    near = MAX_REL + mq
    n_rel = rel_table.shape[-1]
    lead = jnp.broadcast_to(rel_table[..., :1], rel_table.shape[:-1] + (near,))
    body = rel_table[..., :min(near, n_rel)]
    parts = [lead, body]
    if near > n_rel:
        parts.append(jnp.broadcast_to(rel_table[..., -1:], rel_table.shape[:-1] + (near - n_rel,)))
    return jnp.concatenate(parts, axis=-1)[:, :, None, :]


def _attn(q, k, v, table, layer):
    bsz, heads, seq, hd = q.shape
    mq = min(ATT_Q, seq)
    pad = LEFT_CHUNKS * CHUNK
    win = pad + mq
    nh = ATT_HEADS_PER_STEP
    blk = pl.BlockSpec((1, nh, seq, hd), lambda h, b: (b, h, 0, 0))
    return pl.pallas_call(
        functools.partial(_attn_body, seq=seq, mq=mq),
        grid=(heads // nh, bsz),
        in_specs=[blk, blk, blk,
                  pl.BlockSpec((None, nh, 1, table.shape[-1]), lambda h, b: (layer, h, 0, 0))],
        out_specs=blk,
        out_shape=jax.ShapeDtypeStruct((bsz, heads, seq, hd), F32),
        scratch_shapes=[pltpu.VMEM((nh, mq, win), F32), pltpu.VMEM((nh, 2, mq, win), F32),
                        pltpu.VMEM((nh, 2, mq, win), BF16), pltpu.VMEM((nh, 2, mq, hd), F32)],
        compiler_params=_params("parallel", "arbitrary"),
        name="attn",
    )(q, k, v, table)


def _outproj_body(yl_ref, ya_ref, gl_ref, ga_ref, w_ref, x_ref, o_ref):
    half = yl_ref.shape[1]
    yl = _rms(yl_ref[...], gl_ref[...]).astype(BF16)
    att = jnp.concatenate([ya_ref[0, hd] for hd in range(ATT_HEADS)], axis=-1)
    ya = _rms(att, ga_ref[...]).astype(BF16)
    y = _dot(yl, w_ref[0:half, :]) + _dot(ya, w_ref[half:2 * half, :])
    o_ref[...] = x_ref[...] + y


def _out_proj(y_lru, y_att, g_lru, g_att, w, x, layer):
    t, d = x.shape
    half = y_lru.shape[1]
    bsz, heads, seq, hd = y_att.shape
    tm = min(ROW_TILE, seq)
    nsb = seq // tm
    return pl.pallas_call(
        _outproj_body,
        grid=(t // tm,),
        in_specs=[
            pl.BlockSpec((tm, half), lambda i: (i, 0)),
            pl.BlockSpec((1, heads, tm, hd), lambda i: (i // nsb, 0, i % nsb, 0)),
            _layer_vec(half, layer),
            _layer_vec(half, layer),
            pl.BlockSpec((None, 2 * half, d), lambda i: (layer, 0, 0),
                         pipeline_mode=pl.Buffered(1)),
            pl.BlockSpec((tm, d), lambda i: (i, 0)),
        ],
        out_specs=pl.BlockSpec((tm, d), lambda i: (i, 0)),
        out_shape=jax.ShapeDtypeStruct((t, d), F32),
        compiler_params=_params("parallel"),
        name="out_proj",
    )(y_lru, y_att, g_lru, g_att, w, x)


def kernel(x, ffn1_norm, ffn1_w_gate, ffn1_w_up, ffn1_w_down, mix_norm, w_in, conv_w, conv_b, lru_gate_a_w, lru_gate_a_b, lru_gate_x_w, lru_gate_x_b, lru_lambda, rel_bias, lru_out_norm, att_out_norm, w_out, ffn2_norm, ffn2_w_gate, ffn2_w_up, ffn2_w_down, final_norm):
    bsz, seq, d = x.shape
    depth = w_in.shape[0]
    xf = x.reshape(bsz * seq, d)
    bf = lambda w: w.astype(BF16)
    vec = lambda p: p[:, None, :]
    wg1, wu1, wd1 = bf(ffn1_w_gate), bf(ffn1_w_up), bf(ffn1_w_down)
    wg2, wu2, wd2 = bf(ffn2_w_gate), bf(ffn2_w_up), bf(ffn2_w_down)
    w_in_b, w_out_b = bf(w_in), bf(w_out)
    wa, wx = _block_diag(lru_gate_a_w), _block_diag(lru_gate_x_w)
    table = _attn_table(rel_bias, min(ATT_Q, seq))
    n1, nm, n2 = vec(ffn1_norm), vec(mix_norm), vec(ffn2_norm)
    cb, ba, bx, lam = vec(conv_b), vec(lru_gate_a_b), vec(lru_gate_x_b), vec(lru_lambda)
    g_lru, g_att = vec(lru_out_norm), vec(att_out_norm)
    for l in range(depth):
        xf = _ffn(xf, n1, wg1, wu1, wd1, l)
        y_lru, q, k, v = _in_proj(xf, nm, w_in_b, conv_w, cb, wa, ba, wx, bx, lam, l, bsz, seq)
        y_att = _attn(q, k, v, table, l)
        xf = _out_proj(y_lru, y_att, g_lru, g_att, w_out_b, xf, l)
        last = l == depth - 1
        xf = _ffn(xf, n2, wg2, wu2, wd2, l, final_gain=final_norm if last else None)
    return xf.reshape(bsz, seq, d)
```

```python
import functools

import jax
import jax.numpy as jnp
from jax import lax
from jax.experimental import pallas as pl
from jax.experimental.pallas import tpu as pltpu

F32 = jnp.float32
BF16 = jnp.bfloat16

CHUNK = 64
LEFT_CHUNKS = 8
LRU_HEADS = 16
CONV_WIDTH = 4
N_PROJ = 5
LRU_C = 8.0
ATT_HEADS = 8
MAX_REL = 128
EPS = 1e-6
NEG_INF = -1e30
LOG2_E = 1.4426950408889634

LANES = 128
SUBLANES = 8
MXU_DIM = 256
VMEM_LIMIT = 56 * 1024 * 1024

ROW_TILE = 512
FF_ROW_TILE = 1024
FF_TILE = 512
FF_SPLIT = 2
LRU_GROUP = MXU_DIM
LRU_ROWS = 128
ATT_Q = 128
ATT_HEADS_PER_STEP = 2


def _params(*sem):
    return pltpu.CompilerParams(dimension_semantics=sem, vmem_limit_bytes=VMEM_LIMIT)


def _rms(x, g):
    ms = jnp.mean(x * x, axis=-1, keepdims=True)
    return x * lax.rsqrt(ms + EPS) * g


def _dot(a, b):
    return jnp.dot(a, b, preferred_element_type=F32)


def _layer_vec(width, layer):
    return pl.BlockSpec((None, 1, width), lambda *_: (layer, 0, 0))


def _ffn_body(*refs, n_ff, final):
    if final:
        x_ref, g_ref, wg_ref, wu_ref, wd_ref, fg_ref, o_ref, h_ref = refs
    else:
        x_ref, g_ref, wg_ref, wu_ref, wd_ref, o_ref, h_ref = refs
    j = pl.program_id(1)

    def step(first):
        if first:
            h = _rms(x_ref[...], g_ref[...]).astype(BF16)
            h_ref[...] = h
        else:
            h = h_ref[...]
        ts = wg_ref.shape[1] // FF_SPLIT
        for s in range(FF_SPLIT):
            g = _dot(h, wg_ref[:, s * ts:(s + 1) * ts])
            u = _dot(h, wu_ref[:, s * ts:(s + 1) * ts])
            a = (g * jax.nn.sigmoid(g) * (0.5 * u)).astype(BF16)
            d = _dot(a, wd_ref[s * ts:(s + 1) * ts, :])
            if first and s == 0:
                o_ref[...] = x_ref[...] + d
            else:
                o_ref[...] += d

    pl.when(j == 0)(functools.partial(step, True))
    pl.when(j > 0)(functools.partial(step, False))

    if final:
        @pl.when(j == n_ff - 1)
        def _():
            o_ref[...] = _rms(o_ref[...], fg_ref[...])


def _ffn(x, gain, wg, wu, wd, layer, final_gain=None):
    t, d = x.shape
    d_ff = wg.shape[2]
    tm = min(FF_ROW_TILE, t)
    tf = min(FF_TILE, d_ff)
    n_ff = d_ff // tf
    final = final_gain is not None
    in_specs = [
        pl.BlockSpec((tm, d), lambda i, j: (i, 0)),
        _layer_vec(d, layer),
        pl.BlockSpec((None, d, tf), lambda i, j: (layer, 0, j)),
        pl.BlockSpec((None, d, tf), lambda i, j: (layer, 0, j)),
        pl.BlockSpec((None, tf, d), lambda i, j: (layer, j, 0)),
    ]
    args = [x, gain, wg, wu, wd]
    if final:
        in_specs.append(pl.BlockSpec((1, d), lambda i, j: (0, 0)))
        args.append(final_gain.reshape(1, d))
    return pl.pallas_call(
        functools.partial(_ffn_body, n_ff=n_ff, final=final),
        grid=(t // tm, n_ff),
        in_specs=in_specs,
        out_specs=pl.BlockSpec((tm, d), lambda i, j: (i, 0)),
        out_shape=jax.ShapeDtypeStruct((t, d), F32),
        scratch_shapes=[pltpu.VMEM((tm, d), BF16)],
        compiler_params=_params("parallel", "arbitrary"),
        name="ffn_final" if final else "ffn",
    )(*args)


def _lru_rows(zz_ref, gate, cw, cb, wa, ba, wx, bx, neg_c_sp, h0, cols, r0, rows):
    c = LRU_GROUP
    tiles = rows // SUBLANES
    xc = cb
    for k in range(CONV_WIDTH):
        start = r0 + SUBLANES - (CONV_WIDTH - 1 - k)
        xc = xc + zz_ref[start:start + rows, cols] * cw[k:k + 1]

    xb = xc.astype(BF16)
    r = jax.nn.sigmoid(_dot(xb, wa) + ba)
    gi = jax.nn.sigmoid(_dot(xb, wx) + bx)
    log_a = neg_c_sp * r
    a = jnp.exp(log_a)
    mult = jnp.sqrt(-jnp.tanh(log_a) * (1.0 + a * a))
    u = mult * gi * xc

    sub = lax.broadcasted_iota(jnp.int32, (1, SUBLANES, c), 1)
    a = a.reshape(tiles, SUBLANES, c)
    u = u.reshape(tiles, SUBLANES, c)
    for k in (1, 2, 4):
        a_s = jnp.where(sub >= k, pltpu.roll(a, k, 1), 1.0)
        u_s = jnp.where(sub >= k, pltpu.roll(u, k, 1), 0.0)
        u = a * u_s + u
        a = a * a_s

    hs = []
    h = h0
    for t in range(tiles):
        ht = a[t] * h + u[t]
        hs.append(ht)
        h = jnp.broadcast_to(ht[SUBLANES - 1:SUBLANES], (SUBLANES, c))
    return jnp.concatenate(hs, axis=0) * jax.nn.gelu(gate), h


def _inproj_body(x_ref, g_ref, w_ref, cw_ref, cb_ref, wa_ref, ba_ref, wx_ref, bx_ref, lam_ref,
                 y_ref, q_ref, k_ref, v_ref, h_ref, zz_ref, gate_ref, hc_ref, *, nsb):
    tm, width = y_ref.shape

    @pl.when(pl.program_id(0) == 0)
    def _():
        zz_ref[tm:tm + SUBLANES, :] = jnp.zeros((SUBLANES, width), F32)
        hc_ref[...] = jnp.zeros_like(hc_ref)

    h_ref[...] = _rms(x_ref[...], g_ref[...]).astype(BF16)

    def proj(lo, n):
        return _dot(h_ref[...], w_ref[:, lo:lo + n])

    first = pl.program_id(0) % nsb == 0
    zz_ref[0:SUBLANES, :] = jnp.where(first, 0.0, zz_ref[tm:tm + SUBLANES, :])
    neg_c_sp = -LRU_C * jax.nn.softplus(-lam_ref[...])
    ng = width // LRU_GROUP

    def project_group(g):
        cols = slice(g * LRU_GROUP, (g + 1) * LRU_GROUP)
        zz_ref[SUBLANES:SUBLANES + tm, cols] = proj(g * LRU_GROUP, LRU_GROUP)
        gate_ref[:, cols] = proj(width + g * LRU_GROUP, LRU_GROUP)

    def lru_group(g):
        cols = slice(g * LRU_GROUP, (g + 1) * LRU_GROUP)
        h = jnp.where(first, 0.0, hc_ref[:, cols])
        for r0 in range(0, tm, LRU_ROWS):
            y, h = _lru_rows(zz_ref, gate_ref[r0:r0 + LRU_ROWS, cols], cw_ref[:, cols],
                             cb_ref[:, cols], wa_ref[g], ba_ref[:, cols], wx_ref[g],
                             bx_ref[:, cols], neg_c_sp[:, cols], h, cols, r0, LRU_ROWS)
            y_ref[r0:r0 + LRU_ROWS, cols] = y
        hc_ref[:, cols] = h

    def project_heads(idx, ref):
        z = proj(idx * width, width).astype(BF16)
        for hd in range(ATT_HEADS):
            ref[0, hd] = z[:, hd * LANES:(hd + 1) * LANES]

    project_group(0)
    for g in range(ng):
        if g + 1 < ng:
            project_group(g + 1)
        else:
            project_heads(2, q_ref)
        lru_group(g)
    project_heads(3, k_ref)
    project_heads(4, v_ref)


def _in_proj(x, gain, w, conv_w, conv_b, wa, ba, wx, bx, lam, layer, bsz, seq):
    t, d = x.shape
    width = w.shape[2] // N_PROJ
    hd = width // ATT_HEADS
    assert hd == LANES
    tm = min(ROW_TILE, seq)
    nsb = seq // tm
    ng = width // LRU_GROUP
    vec = _layer_vec(width, layer)
    mat = pl.BlockSpec((None, ng, LRU_GROUP, LRU_GROUP), lambda i: (layer, 0, 0, 0))
    head = pl.BlockSpec((1, ATT_HEADS, tm, hd), lambda i: (i // nsb, 0, i % nsb, 0))
    heads = jax.ShapeDtypeStruct((bsz, ATT_HEADS, seq, hd), BF16)
    return pl.pallas_call(
        functools.partial(_inproj_body, nsb=nsb),
        grid=(t // tm,),
        in_specs=[
            pl.BlockSpec((tm, d), lambda i: (i, 0)),
            _layer_vec(d, layer),
            pl.BlockSpec((None, d, N_PROJ * width), lambda i: (layer, 0, 0),
                         pipeline_mode=pl.Buffered(1)),
            pl.BlockSpec((None, CONV_WIDTH, width), lambda i: (layer, 0, 0)),
            vec, mat, vec, mat, vec, vec,
        ],
        out_specs=[pl.BlockSpec((tm, width), lambda i: (i, 0)), head, head, head],
        out_shape=[jax.ShapeDtypeStruct((t, width), F32), heads, heads, heads],
        scratch_shapes=[pltpu.VMEM((tm, d), BF16),
                        pltpu.VMEM((tm + SUBLANES, width), F32),
                        pltpu.VMEM((tm, width), F32),
                        pltpu.VMEM((SUBLANES, width), F32)],
        compiler_params=_params("arbitrary"),
        name="in_proj",
    )(x, gain, w, conv_w, conv_b, wa, ba, wx, bx, lam)


def _block_diag(w):
    depth, heads, n, _ = w.shape
    per = LRU_GROUP // n
    w5 = w.reshape(depth, heads // per, per, n, n)
    eye = jnp.eye(per, dtype=w.dtype)
    bd = w5[:, :, :, :, None, :] * eye[None, None, :, None, :, None]
    return bd.reshape(depth, heads // per, LRU_GROUP, LRU_GROUP).astype(BF16)


def _attn_body(q_ref, k_ref, v_ref, e_ref, o_ref, bias_ref, s_ref, p_ref, l_ref, *, seq, mq):
    pad = LEFT_CHUNKS * CHUNK
    win = pad + mq
    near = MAX_REL + mq
    nh, hd = q_ref.shape[1], q_ref.shape[-1]
    scale = hd ** -0.5
    nblk = seq // mq

    @pl.when(pl.program_id(1) == 0)
    def _():
        qc = lax.broadcasted_iota(jnp.int32, (mq, win), 0) // CHUNK
        kc = lax.broadcasted_iota(jnp.int32, (mq, win), 1) // CHUNK - LEFT_CHUNKS
        band = (kc <= qc) & (kc >= qc - LEFT_CHUNKS)
        for h in range(nh):
            e = e_ref[h] * (1.0 / scale)
            shifted = pltpu.roll(jnp.broadcast_to(e, (mq, 2 * near)), 0, 1,
                                 stride=1, stride_axis=0)
            far = jnp.broadcast_to(e[:, 0:1], (mq, win - near))
            bias = jnp.concatenate([far, shifted[:, near:]], axis=1)
            bias_ref[h] = jnp.where(band, bias, NEG_INF)

    def span(k):
        hi = (k + 1) * mq
        lo = max(hi - win, 0)
        return lo, hi

    def scores(h, k):
        lo, hi = span(k)
        q = q_ref[0, h, k * mq:(k + 1) * mq, :]
        s_ref[h, k % 2, :, 0:hi - lo] = lax.dot_general(
            q, k_ref[0, h, lo:hi, :], (((1,), (1,)), ((), ())), preferred_element_type=F32)

    def softmax(h, k):
        lo, hi = span(k)
        t = s_ref[h, k % 2, :, 0:hi - lo] + bias_ref[h, :, win - (hi - lo):win]
        m = jnp.max(t, axis=-1, keepdims=True)
        p = jnp.exp2((t - m) * (scale * LOG2_E))
        p_ref[h, k % 2, :, 0:hi - lo] = p.astype(BF16)
        l_ref[h, k % 2] = jnp.broadcast_to(jnp.sum(p, axis=-1, keepdims=True), (mq, hd))

    def values(h, k):
        lo, hi = span(k)
        pv = _dot(p_ref[h, k % 2, :, 0:hi - lo], v_ref[0, h, lo:hi, :])
        o_ref[0, h, k * mq:(k + 1) * mq, :] = pv / l_ref[h, k % 2]

    for k in range(nblk + 2):
        for h in range(nh):
            if k < nblk:
                scores(h, k)
            if 1 <= k <= nblk:
                softmax(h, k - 1)
            if k >= 2:
                values(h, k - 2)


def _attn_table(rel_table, mq):
    near = MAX_REL + mq
    n_rel = rel_table.shape[-1]
    lead = jnp.broadcast_to(rel_table[..., :1], rel_table.shape[:-1] + (near,))
    body = rel_table[..., :min(near, n_rel)]
    parts = [lead, body]
    if near > n_rel:
        parts.append(jnp.broadcast_to(rel_table[..., -1:], rel_table.shape[:-1] + (near - n_rel,)))
    return jnp.concatenate(parts, axis=-1)[:, :, None, :]


def _attn(q, k, v, table, layer):
    bsz, heads, seq, hd = q.shape
    mq = min(ATT_Q, seq)
    pad = LEFT_CHUNKS * CHUNK
    win = pad + mq
    nh = ATT_HEADS_PER_STEP
    blk = pl.BlockSpec((1, nh, seq, hd), lambda h, b: (b, h, 0, 0))
    return pl.pallas_call(
        functools.partial(_attn_body, seq=seq, mq=mq),
        grid=(heads // nh, bsz),
        in_specs=[blk, blk, blk,
                  pl.BlockSpec((None, nh, 1, table.shape[-1]), lambda h, b: (layer, h, 0, 0))],
        out_specs=blk,
        out_shape=jax.ShapeDtypeStruct((bsz, heads, seq, hd), F32),
        scratch_shapes=[pltpu.VMEM((nh, mq, win), F32), pltpu.VMEM((nh, 2, mq, win), F32),
                        pltpu.VMEM((nh, 2, mq, win), BF16), pltpu.VMEM((nh, 2, mq, hd), F32)],
        compiler_params=_params("parallel", "arbitrary"),
        name="attn",
    )(q, k, v, table)


def _outproj_body(yl_ref, ya_ref, gl_ref, ga_ref, w_ref, x_ref, o_ref):
    half = yl_ref.shape[1]
    yl = _rms(yl_ref[...], gl_ref[...]).astype(BF16)
    att = jnp.concatenate([ya_ref[0, hd] for hd in range(ATT_HEADS)], axis=-1)
    ya = _rms(att, ga_ref[...]).astype(BF16)
    y = _dot(yl, w_ref[0:half, :]) + _dot(ya, w_ref[half:2 * half, :])
    o_ref[...] = x_ref[...] + y


def _out_proj(y_lru, y_att, g_lru, g_att, w, x, layer):
    t, d = x.shape
    half = y_lru.shape[1]
    bsz, heads, seq, hd = y_att.shape
    tm = min(ROW_TILE, seq)
    nsb = seq // tm
    return pl.pallas_call(
        _outproj_body,
        grid=(t // tm,),
        in_specs=[
            pl.BlockSpec((tm, half), lambda i: (i, 0)),
            pl.BlockSpec((1, heads, tm, hd), lambda i: (i // nsb, 0, i % nsb, 0)),
            _layer_vec(half, layer),
            _layer_vec(half, layer),
            pl.BlockSpec((None, 2 * half, d), lambda i: (layer, 0, 0),
                         pipeline_mode=pl.Buffered(1)),
            pl.BlockSpec((tm, d), lambda i: (i, 0)),
        ],
        out_specs=pl.BlockSpec((tm, d), lambda i: (i, 0)),
        out_shape=jax.ShapeDtypeStruct((t, d), F32),
        compiler_params=_params("parallel"),
        name="out_proj",
    )(y_lru, y_att, g_lru, g_att, w, x)


def kernel(x, ffn1_norm, ffn1_w_gate, ffn1_w_up, ffn1_w_down, mix_norm, w_in, conv_w, conv_b, lru_gate_a_w, lru_gate_a_b, lru_gate_x_w, lru_gate_x_b, lru_lambda, rel_bias, lru_out_norm, att_out_norm, w_out, ffn2_norm, ffn2_w_gate, ffn2_w_up, ffn2_w_down, final_norm):
    bsz, seq, d = x.shape
    depth = w_in.shape[0]
    xf = x.reshape(bsz * seq, d)
    bf = lambda w: w.astype(BF16)
    vec = lambda p: p[:, None, :]
    wg1, wu1, wd1 = bf(ffn1_w_gate), bf(ffn1_w_up), bf(ffn1_w_down)
    wg2, wu2, wd2 = bf(ffn2_w_gate), bf(ffn2_w_up), bf(ffn2_w_down)
    w_in_b, w_out_b = bf(w_in), bf(w_out)
    wa, wx = _block_diag(lru_gate_a_w), _block_diag(lru_gate_x_w)
    table = _attn_table(rel_bias, min(ATT_Q, seq))
    n1, nm, n2 = vec(ffn1_norm), vec(mix_norm), vec(ffn2_norm)
    cb, ba, bx, lam = vec(conv_b), vec(lru_gate_a_b), vec(lru_gate_x_b), vec(lru_lambda)
    g_lru, g_att = vec(lru_out_norm), vec(att_out_norm)
    for l in range(depth):
        xf = _ffn(xf, n1, wg1, wu1, wd1, l)
        y_lru, q, k, v = _in_proj(xf, nm, w_in_b, conv_w, cb, wa, ba, wx, bx, lam, l, bsz, seq)
        y_att = _attn(q, k, v, table, l)
        xf = _out_proj(y_lru, y_att, g_lru, g_att, w_out_b, xf, l)
        last = l == depth - 1
        xf = _ffn(xf, n2, wg2, wu2, wd2, l, final_gain=final_norm if last else None)
    return xf.reshape(bsz, seq, d)
```

```python
import functools

import jax
import jax.numpy as jnp
from jax import lax
from jax.experimental import pallas as pl
from jax.experimental.pallas import tpu as pltpu

F32 = jnp.float32
BF16 = jnp.bfloat16

CHUNK = 64
LEFT_CHUNKS = 8
LRU_HEADS = 16
CONV_WIDTH = 4
N_PROJ = 5
LRU_C = 8.0
ATT_HEADS = 8
MAX_REL = 128
EPS = 1e-6
NEG_INF = -1e30
LOG2_E = 1.4426950408889634

LANES = 128
SUBLANES = 8
MXU_DIM = 256
VMEM_LIMIT = 56 * 1024 * 1024

ROW_TILE = 512
FF_ROW_TILE = 1024
FF_TILE = 512
FF_SPLIT = 2
LRU_GROUP = MXU_DIM
LRU_ROWS = 128
ATT_Q = 128
ATT_HEADS_PER_STEP = 2


def _params(*sem):
    return pltpu.CompilerParams(dimension_semantics=sem, vmem_limit_bytes=VMEM_LIMIT)


def _rms(x, g):
    ms = jnp.mean(x * x, axis=-1, keepdims=True)
    return x * lax.rsqrt(ms + EPS) * g


def _dot(a, b):
    return jnp.dot(a, b, preferred_element_type=F32)


def _layer_vec(width, layer):
    return pl.BlockSpec((None, 1, width), lambda *_: (layer, 0, 0))


def _ffn_body(*refs, n_ff, final):
    if final:
        x_ref, g_ref, wg_ref, wu_ref, wd_ref, fg_ref, o_ref, h_ref = refs
    else:
        x_ref, g_ref, wg_ref, wu_ref, wd_ref, o_ref, h_ref = refs
    j = pl.program_id(1)

    def step(first):
        if first:
            h = _rms(x_ref[...], g_ref[...]).astype(BF16)
            h_ref[...] = h
        else:
            h = h_ref[...]
        ts = wg_ref.shape[1] // FF_SPLIT
        for s in range(FF_SPLIT):
            g = _dot(h, wg_ref[:, s * ts:(s + 1) * ts])
            u = _dot(h, wu_ref[:, s * ts:(s + 1) * ts])
            a = (g * jax.nn.sigmoid(g) * (0.5 * u)).astype(BF16)
            d = _dot(a, wd_ref[s * ts:(s + 1) * ts, :])
            if first and s == 0:
                o_ref[...] = x_ref[...] + d
            else:
                o_ref[...] += d

    pl.when(j == 0)(functools.partial(step, True))
    pl.when(j > 0)(functools.partial(step, False))

    if final:
        @pl.when(j == n_ff - 1)
        def _():
            o_ref[...] = _rms(o_ref[...], fg_ref[...])


def _col_tiles(w):
    depth, d, d_ff = w.shape
    tf = min(FF_TILE, d_ff)
    return w.reshape(depth, d, d_ff // tf, tf).transpose(0, 2, 1, 3).astype(BF16)


def _ffn(x, gain, wg, wu, wd, layer, final_gain=None):
    t, d = x.shape
    _, n_ff, _, tf = wg.shape
    tm = min(FF_ROW_TILE, t)
    final = final_gain is not None
    in_specs = [
        pl.BlockSpec((tm, d), lambda i, j: (i, 0)),
        _layer_vec(d, layer),
        pl.BlockSpec((None, None, d, tf), lambda i, j: (layer, j, 0, 0)),
        pl.BlockSpec((None, None, d, tf), lambda i, j: (layer, j, 0, 0)),
        pl.BlockSpec((None, tf, d), lambda i, j: (layer, j, 0)),
    ]
    args = [x, gain, wg, wu, wd]
    if final:
        in_specs.append(pl.BlockSpec((1, d), lambda i, j: (0, 0)))
        args.append(final_gain.reshape(1, d))
    return pl.pallas_call(
        functools.partial(_ffn_body, n_ff=n_ff, final=final),
        grid=(t // tm, n_ff),
        in_specs=in_specs,
        out_specs=pl.BlockSpec((tm, d), lambda i, j: (i, 0)),
        out_shape=jax.ShapeDtypeStruct((t, d), F32),
        scratch_shapes=[pltpu.VMEM((tm, d), BF16)],
        compiler_params=_params("parallel", "arbitrary"),
        name="ffn_final" if final else "ffn",
    )(*args)


def _lru_rows(zz_ref, gate, cw, cb, wa, ba, wx, bx, neg_c_sp, h0, cols, r0, rows):
    c = LRU_GROUP
    tiles = rows // SUBLANES
    xc = cb
    for k in range(CONV_WIDTH):
        start = r0 + SUBLANES - (CONV_WIDTH - 1 - k)
        xc = xc + zz_ref[start:start + rows, cols] * cw[k:k + 1]

    xb = xc.astype(BF16)
    r = jax.nn.sigmoid(_dot(xb, wa) + ba)
    gi = jax.nn.sigmoid(_dot(xb, wx) + bx)
    log_a = neg_c_sp * r
    a = jnp.exp(log_a)
    mult = jnp.sqrt(-jnp.tanh(log_a) * (1.0 + a * a))
    u = mult * gi * xc

    sub = lax.broadcasted_iota(jnp.int32, (1, SUBLANES, c), 1)
    a = a.reshape(tiles, SUBLANES, c)
    u = u.reshape(tiles, SUBLANES, c)
    for k in (1, 2, 4):
        a_s = jnp.where(sub >= k, pltpu.roll(a, k, 1), 1.0)
        u_s = jnp.where(sub >= k, pltpu.roll(u, k, 1), 0.0)
        u = a * u_s + u
        a = a * a_s

    hs = []
    h = h0
    for t in range(tiles):
        ht = a[t] * h + u[t]
        hs.append(ht)
        h = jnp.broadcast_to(ht[SUBLANES - 1:SUBLANES], (SUBLANES, c))
    return jnp.concatenate(hs, axis=0) * jax.nn.gelu(gate), h


def _inproj_body(x_ref, g_ref, w_ref, cw_ref, cb_ref, wa_ref, ba_ref, wx_ref, bx_ref, lam_ref,
                 y_ref, q_ref, k_ref, v_ref, h_ref, zz_ref, gate_ref, hc_ref, *, nsb):
    tm, width = y_ref.shape

    @pl.when(pl.program_id(0) == 0)
    def _():
        zz_ref[tm:tm + SUBLANES, :] = jnp.zeros((SUBLANES, width), F32)
        hc_ref[...] = jnp.zeros_like(hc_ref)

    h_ref[...] = _rms(x_ref[...], g_ref[...]).astype(BF16)

    def proj(lo, n):
        return _dot(h_ref[...], w_ref[:, lo:lo + n])

    first = pl.program_id(0) % nsb == 0
    zz_ref[0:SUBLANES, :] = jnp.where(first, 0.0, zz_ref[tm:tm + SUBLANES, :])
    neg_c_sp = -LRU_C * jax.nn.softplus(-lam_ref[...])
    ng = width // LRU_GROUP

    def project_group(g):
        cols = slice(g * LRU_GROUP, (g + 1) * LRU_GROUP)
        zz_ref[SUBLANES:SUBLANES + tm, cols] = proj(g * LRU_GROUP, LRU_GROUP)
        gate_ref[:, cols] = proj(width + g * LRU_GROUP, LRU_GROUP)

    def lru_group(g):
        cols = slice(g * LRU_GROUP, (g + 1) * LRU_GROUP)
        h = jnp.where(first, 0.0, hc_ref[:, cols])
        for r0 in range(0, tm, LRU_ROWS):
            y, h = _lru_rows(zz_ref, gate_ref[r0:r0 + LRU_ROWS, cols], cw_ref[:, cols],
                             cb_ref[:, cols], wa_ref[g], ba_ref[:, cols], wx_ref[g],
                             bx_ref[:, cols], neg_c_sp[:, cols], h, cols, r0, LRU_ROWS)
            y_ref[r0:r0 + LRU_ROWS, cols] = y
        hc_ref[:, cols] = h

    def project_heads(idx, ref):
        z = proj(idx * width, width).astype(BF16)
        for hd in range(ATT_HEADS):
            ref[0, hd] = z[:, hd * LANES:(hd + 1) * LANES]

    project_group(0)
    for g in range(ng):
        if g + 1 < ng:
            project_group(g + 1)
        else:
            project_heads(2, q_ref)
        lru_group(g)
    project_heads(3, k_ref)
    project_heads(4, v_ref)


def _in_proj(x, gain, w, conv_w, conv_b, wa, ba, wx, bx, lam, layer, bsz, seq):
    t, d = x.shape
    width = w.shape[2] // N_PROJ
    hd = width // ATT_HEADS
    assert hd == LANES
    tm = min(ROW_TILE, seq)
    nsb = seq // tm
    ng = width // LRU_GROUP
    vec = _layer_vec(width, layer)
    mat = pl.BlockSpec((None, ng, LRU_GROUP, LRU_GROUP), lambda i: (layer, 0, 0, 0))
    head = pl.BlockSpec((1, ATT_HEADS, tm, hd), lambda i: (i // nsb, 0, i % nsb, 0))
    heads = jax.ShapeDtypeStruct((bsz, ATT_HEADS, seq, hd), BF16)
    return pl.pallas_call(
        functools.partial(_inproj_body, nsb=nsb),
        grid=(t // tm,),
        in_specs=[
            pl.BlockSpec((tm, d), lambda i: (i, 0)),
            _layer_vec(d, layer),
            pl.BlockSpec((None, d, N_PROJ * width), lambda i: (layer, 0, 0),
                         pipeline_mode=pl.Buffered(1)),
            pl.BlockSpec((None, CONV_WIDTH, width), lambda i: (layer, 0, 0)),
            vec, mat, vec, mat, vec, vec,
        ],
        out_specs=[pl.BlockSpec((tm, width), lambda i: (i, 0)), head, head, head],
        out_shape=[jax.ShapeDtypeStruct((t, width), F32), heads, heads, heads],
        scratch_shapes=[pltpu.VMEM((tm, d), BF16),
                        pltpu.VMEM((tm + SUBLANES, width), F32),
                        pltpu.VMEM((tm, width), F32),
                        pltpu.VMEM((SUBLANES, width), F32)],
        compiler_params=_params("arbitrary"),
        name="in_proj",
    )(x, gain, w, conv_w, conv_b, wa, ba, wx, bx, lam)


def _block_diag(w):
    depth, heads, n, _ = w.shape
    per = LRU_GROUP // n
    w5 = w.reshape(depth, heads // per, per, n, n)
    eye = jnp.eye(per, dtype=w.dtype)
    bd = w5[:, :, :, :, None, :] * eye[None, None, :, None, :, None]
    return bd.reshape(depth, heads // per, LRU_GROUP, LRU_GROUP).astype(BF16)


def _attn_body(q_ref, k_ref, v_ref, e_ref, o_ref, bias_ref, s_ref, p_ref, l_ref, *, seq, mq):
    pad = LEFT_CHUNKS * CHUNK
    win = pad + mq
    near = MAX_REL + mq
    nh, hd = q_ref.shape[1], q_ref.shape[-1]
    scale = hd ** -0.5
    nblk = seq // mq

    @pl.when(pl.program_id(1) == 0)
    def _():
        qc = lax.broadcasted_iota(jnp.int32, (mq, win), 0) // CHUNK
        kc = lax.broadcasted_iota(jnp.int32, (mq, win), 1) // CHUNK - LEFT_CHUNKS
        band = (kc <= qc) & (kc >= qc - LEFT_CHUNKS)
        for h in range(nh):
            e = e_ref[h] * (1.0 / scale)
            shifted = pltpu.roll(jnp.broadcast_to(e, (mq, 2 * near)), 0, 1,
                                 stride=1, stride_axis=0)
            far = jnp.broadcast_to(e[:, 0:1], (mq, win - near))
            bias = jnp.concatenate([far, shifted[:, near:]], axis=1)
            bias_ref[h] = jnp.where(band, bias, NEG_INF)

    def span(k):
        hi = (k + 1) * mq
        lo = max(hi - win, 0)
        return lo, hi

    def scores(h, k):
        lo, hi = span(k)
        q = q_ref[0, h, k * mq:(k + 1) * mq, :]
        s_ref[h, k % 2, :, 0:hi - lo] = lax.dot_general(
            q, k_ref[0, h, lo:hi, :], (((1,), (1,)), ((), ())), preferred_element_type=F32)

    def softmax(h, k):
        lo, hi = span(k)
        t = s_ref[h, k % 2, :, 0:hi - lo] + bias_ref[h, :, win - (hi - lo):win]
        m = jnp.max(t, axis=-1, keepdims=True)
        p = jnp.exp2((t - m) * (scale * LOG2_E))
        p_ref[h, k % 2, :, 0:hi - lo] = p.astype(BF16)
        l_ref[h, k % 2] = jnp.broadcast_to(jnp.sum(p, axis=-1, keepdims=True), (mq, hd))

    def values(h, k):
        lo, hi = span(k)
        pv = _dot(p_ref[h, k % 2, :, 0:hi - lo], v_ref[0, h, lo:hi, :])
        o_ref[0, h, k * mq:(k + 1) * mq, :] = pv / l_ref[h, k % 2]

    for k in range(nblk + 2):
        for h in range(nh):
            if k < nblk:
                scores(h, k)
            if 1 <= k <= nblk:
                softmax(h, k - 1)
            if k >= 2:
                values(h, k - 2)


def _attn_table(rel_table, mq):
    near = MAX_REL + mq
    n_rel = rel_table.shape[-1]
    lead = jnp.broadcast_to(rel_table[..., :1], rel_table.shape[:-1] + (near,))
    body = rel_table[..., :min(near, n_rel)]
    parts = [lead, body]
    if near > n_rel:
        parts.append(jnp.broadcast_to(rel_table[..., -1:], rel_table.shape[:-1] + (near - n_rel,)))
    return jnp.concatenate(parts, axis=-1)[:, :, None, :]


def _attn(q, k, v, table, layer):
    bsz, heads, seq, hd = q.shape
    mq = min(ATT_Q, seq)
    pad = LEFT_CHUNKS * CHUNK
    win = pad + mq
    nh = ATT_HEADS_PER_STEP
    blk = pl.BlockSpec((1, nh, seq, hd), lambda h, b: (b, h, 0, 0))
    return pl.pallas_call(
        functools.partial(_attn_body, seq=seq, mq=mq),
        grid=(heads // nh, bsz),
        in_specs=[blk, blk, blk,
                  pl.BlockSpec((None, nh, 1, table.shape[-1]), lambda h, b: (layer, h, 0, 0))],
        out_specs=blk,
        out_shape=jax.ShapeDtypeStruct((bsz, heads, seq, hd), F32),
        scratch_shapes=[pltpu.VMEM((nh, mq, win), F32), pltpu.VMEM((nh, 2, mq, win), F32),
                        pltpu.VMEM((nh, 2, mq, win), BF16), pltpu.VMEM((nh, 2, mq, hd), F32)],
        compiler_params=_params("parallel", "arbitrary"),
        name="attn",
    )(q, k, v, table)


def _outproj_body(yl_ref, ya_ref, gl_ref, ga_ref, w_ref, x_ref, o_ref):
    half = yl_ref.shape[1]
    yl = _rms(yl_ref[...], gl_ref[...]).astype(BF16)
    att = jnp.concatenate([ya_ref[0, hd] for hd in range(ATT_HEADS)], axis=-1)
    ya = _rms(att, ga_ref[...]).astype(BF16)
    y = _dot(yl, w_ref[0:half, :]) + _dot(ya, w_ref[half:2 * half, :])
    o_ref[...] = x_ref[...] + y


def _out_proj(y_lru, y_att, g_lru, g_att, w, x, layer):
    t, d = x.shape
    half = y_lru.shape[1]
    bsz, heads, seq, hd = y_att.shape
    tm = min(ROW_TILE, seq)
    nsb = seq // tm
    return pl.pallas_call(
        _outproj_body,
        grid=(t // tm,),
        in_specs=[
            pl.BlockSpec((tm, half), lambda i: (i, 0)),
            pl.BlockSpec((1, heads, tm, hd), lambda i: (i // nsb, 0, i % nsb, 0)),
            _layer_vec(half, layer),
            _layer_vec(half, layer),
            pl.BlockSpec((None, 2 * half, d), lambda i: (layer, 0, 0),
                         pipeline_mode=pl.Buffered(1)),
            pl.BlockSpec((tm, d), lambda i: (i, 0)),
        ],
        out_specs=pl.BlockSpec((tm, d), lambda i: (i, 0)),
        out_shape=jax.ShapeDtypeStruct((t, d), F32),
        compiler_params=_params("parallel"),
        name="out_proj",
    )(y_lru, y_att, g_lru, g_att, w, x)


def kernel(x, ffn1_norm, ffn1_w_gate, ffn1_w_up, ffn1_w_down, mix_norm, w_in, conv_w, conv_b, lru_gate_a_w, lru_gate_a_b, lru_gate_x_w, lru_gate_x_b, lru_lambda, rel_bias, lru_out_norm, att_out_norm, w_out, ffn2_norm, ffn2_w_gate, ffn2_w_up, ffn2_w_down, final_norm):
    bsz, seq, d = x.shape
    depth = w_in.shape[0]
    xf = x.reshape(bsz * seq, d)
    bf = lambda w: w.astype(BF16)
    vec = lambda p: p[:, None, :]
    wg1, wu1, wd1 = _col_tiles(ffn1_w_gate), _col_tiles(ffn1_w_up), bf(ffn1_w_down)
    wg2, wu2, wd2 = _col_tiles(ffn2_w_gate), _col_tiles(ffn2_w_up), bf(ffn2_w_down)
    w_in_b, w_out_b = bf(w_in), bf(w_out)
    wa, wx = _block_diag(lru_gate_a_w), _block_diag(lru_gate_x_w)
    table = _attn_table(rel_bias, min(ATT_Q, seq))
    n1, nm, n2 = vec(ffn1_norm), vec(mix_norm), vec(ffn2_norm)
    cb, ba, bx, lam = vec(conv_b), vec(lru_gate_a_b), vec(lru_gate_x_b), vec(lru_lambda)
    g_lru, g_att = vec(lru_out_norm), vec(att_out_norm)
    for l in range(depth):
        xf = _ffn(xf, n1, wg1, wu1, wd1, l)
        y_lru, q, k, v = _in_proj(xf, nm, w_in_b, conv_w, cb, wa, ba, wx, bx, lam, l, bsz, seq)
        y_att = _attn(q, k, v, table, l)
        xf = _out_proj(y_lru, y_att, g_lru, g_att, w_out_b, xf, l)
        last = l == depth - 1
        xf = _ffn(xf, n2, wg2, wu2, wd2, l, final_gain=final_norm if last else None)
    return xf.reshape(bsz, seq, d)
```
